```python
import numpy as np
import jax
import jax.numpy as jnp
from jax import lax

D_MODEL = 1024
BATCH = 32
SEQ = 2048
DEPTH = 4

N_MIXERS = 4
PLE_DIM = 256
D_FF = 4 * D_MODEL
NORM_EPS = 1e-6

A_HEADS = 8
A_DV = D_MODEL // A_HEADS
A_DQK = A_DV // 2
A_CHUNK = 128
A_SPLITS = (A_HEADS * A_DQK, A_HEADS * A_DQK, D_MODEL, D_MODEL, A_HEADS, A_HEADS)

B_HEADS = 8
B_DK = 128
B_DV = D_MODEL // B_HEADS
B_FDIM = B_HEADS * B_DK
B_CHUNK = 16
B_SPLITS = (B_FDIM, B_FDIM, D_MODEL, D_MODEL)

C_HALF = 2 * D_MODEL
C_GROUPS = 8
C_GROUP_W = C_HALF // C_GROUPS
C_CHUNK = 128

D_RNN = D_MODEL
D_BLOCK_W = 256
D_BLOCKS = D_RNN // D_BLOCK_W
D_CONV = 4
RG_C = 8.0

kernel_name = 'hybrid_mlstm_hgrn2_gmlp_rglru_trunk'


def _rmsnorm(x, gain):
    x32 = x.astype(jnp.float32)
    y = x32 * lax.rsqrt(jnp.mean(x32 * x32, axis=-1, keepdims=True) + NORM_EPS)
    return (y * gain.astype(jnp.float32)).astype(x.dtype)


def _split(z, sizes):
    idx = np.cumsum(sizes)[:-1].tolist()
    return jnp.split(z, idx, axis=-1)


def _to_chunks(t, size):
    b, s, h = t.shape[:3]
    t = t.reshape((b, s // size, size, h) + t.shape[3:])
    return jnp.transpose(t, (1, 0, 3, 2) + tuple(range(4, t.ndim)))


def _from_chunks(t):
    t = jnp.transpose(t, (1, 0, 3, 2) + tuple(range(4, t.ndim)))
    b, nc, size, h = t.shape[:4]
    return t.reshape((b, nc * size, h) + t.shape[4:])


def _mlstm_chunkwise(q, k, v, log_i, log_f):
    bsz, _, nh, dk = q.shape
    dv = v.shape[-1]
    L = A_CHUNK
    mask = jnp.tril(jnp.ones((L, L), dtype=bool))

    def step(carry, inp):
        c_st, n_st, m_st = carry
        qc, kc, vc, ic, fc = inp
        b = jnp.cumsum(fc, axis=-1)
        inter = b + m_st[..., None]
        dmat = jnp.where(mask, b[..., :, None] - b[..., None, :] + ic[..., None, :], -jnp.inf)
        m_t = jnp.maximum(inter, jnp.max(dmat, axis=-1))
        s = jnp.einsum('bhtd,bhsd->bhts', qc, kc) * jnp.exp(dmat - m_t[..., None])
        w_inter = jnp.exp(inter - m_t)
        num = jnp.einsum('bhts,bhsv->bhtv', s, vc) + w_inter[..., None] * jnp.einsum('bhtd,bhdv->bhtv', qc, c_st)
        den = jnp.sum(s, axis=-1) + w_inter * jnp.einsum('bhtd,bhd->bht', qc, n_st)
        out = num / jnp.maximum(jnp.abs(den), jnp.exp(-m_t))[..., None]
        b_last = b[..., -1]
        tail = b_last[..., None] - b + ic
        m_new = jnp.maximum(b_last + m_st, jnp.max(tail, axis=-1))
        decay = jnp.exp(b_last + m_st - m_new)
        wk = jnp.exp(tail - m_new[..., None])
        c_new = decay[..., None, None] * c_st + jnp.einsum('bhs,bhsd,bhsv->bhdv', wk, kc, vc)
        n_new = decay[..., None] * n_st + jnp.einsum('bhs,bhsd->bhd', wk, kc)
        return (c_new, n_new, m_new), out

    init = (jnp.zeros((bsz, nh, dk, dv), jnp.float32),
            jnp.zeros((bsz, nh, dk), jnp.float32),
            jnp.zeros((bsz, nh), jnp.float32))
    xs = (_to_chunks(q, L), _to_chunks(k, L), _to_chunks(v, L), _to_chunks(log_i, L), _to_chunks(log_f, L))
    _, out = lax.scan(step, init, xs)
    return _from_chunks(out)


def _mlstm_mixer(x, w_in, ig_bias, fg_bias, head_gain, w_out):
    bsz, s, _ = x.shape
    f32 = jnp.float32
    q, k, v, o, ig, fg = _split(x @ w_in, A_SPLITS)
    q = q.reshape(bsz, s, A_HEADS, A_DQK).astype(f32) * (A_DQK ** -0.5)
    k = k.reshape(bsz, s, A_HEADS, A_DQK).astype(f32)
    v = v.reshape(bsz, s, A_HEADS, A_DV).astype(f32)
    log_i = ig.astype(f32) + ig_bias.astype(f32)
    log_f = jax.nn.log_sigmoid(fg.astype(f32) + fg_bias.astype(f32))
    hcell = _rmsnorm(_mlstm_chunkwise(q, k, v, log_i, log_f), head_gain)
    y = jax.nn.sigmoid(o.astype(f32)) * hcell.reshape(bsz, s, D_MODEL)
    return y.astype(x.dtype) @ w_out


def _hgrn2_chunkwise(q, k, v, log_f):
    bsz, _, nh, dk = q.shape
    dv = v.shape[-1]
    L = B_CHUNK
    mask = jnp.tril(jnp.ones((L, L), dtype=bool))[:, :, None]

    def step(s_st, inp):
        qc, kc, vc, gc = inp
        g = jnp.cumsum(gc, axis=-2)
        decay = jnp.exp(jnp.where(mask, g[..., :, None, :] - g[..., None, :, :], -jnp.inf))
        attn = jnp.einsum('bhtc,bhsc,bhtsc->bhts', qc, kc, decay)
        out = jnp.einsum('bhts,bhsv->bhtv', attn, vc) + jnp.einsum('bhtc,bhcv->bhtv', qc * jnp.exp(g), s_st)
        g_last = g[..., -1:, :]
        s_new = (jnp.exp(g_last[..., 0, :])[..., None] * s_st
                 + jnp.einsum('bhsc,bhsv->bhcv', kc * jnp.exp(g_last - g), vc))
        return s_new, out

    init = jnp.zeros((bsz, nh, dk, dv), jnp.float32)
    xs = (_to_chunks(q, L), _to_chunks(k, L), _to_chunks(v, L), _to_chunks(log_f, L))
    _, out = lax.scan(step, init, xs)
    return _from_chunks(out)


def _hgrn2_mixer(x, w_in, lower_bound, head_gain, w_out):
    bsz, s, _ = x.shape
    f32 = jnp.float32
    q, fz, i_in, g = _split(x @ w_in, B_SPLITS)
    fz = fz.astype(f32)
    lb = lower_bound.astype(f32)
    log_f = jnp.logaddexp(jnp.log(lb), jnp.log1p(-lb) + jax.nn.log_sigmoid(fz))
    key = (1.0 - lb) * jax.nn.sigmoid(-fz)
    shp = (bsz, s, B_HEADS, B_DK)
    o = _hgrn2_chunkwise(q.astype(f32).reshape(shp), key.reshape(shp),
                         i_in.astype(f32).reshape(bsz, s, B_HEADS, B_DV), log_f.reshape(shp))
    o = _rmsnorm(o, head_gain).reshape(bsz, s, D_MODEL) * jax.nn.silu(g.astype(f32))
    return o.astype(x.dtype) @ w_out


def _gmlp_mixer(x, w_in, ln_gain, ln_bias, w_s, b_s, w_out):
    bsz, s, _ = x.shape
    f32 = jnp.float32
    u, v = jnp.split(jax.nn.gelu(x @ w_in), 2, axis=-1)
    v32 = v.astype(f32)
    mu = jnp.mean(v32, axis=-1, keepdims=True)
    var = jnp.mean(jnp.square(v32 - mu), axis=-1, keepdims=True)
    v32 = (v32 - mu) * lax.rsqrt(var + NORM_EPS) * ln_gain.astype(f32) + ln_bias.astype(f32)
    vc = v32.reshape(bsz, s // C_CHUNK, C_CHUNK, C_GROUPS, C_GROUP_W)
    w_causal = jnp.tril(w_s.astype(f32))
    vm = jnp.einsum('gts,bnsgc->bntgc', w_causal, vc) + b_s.astype(f32).T[:, :, None]
    y = u.astype(f32) * vm.reshape(bsz, s, C_HALF)
    return y.astype(x.dtype) @ w_out


def _linrec_combine(e1, e2):
    a1, b1 = e1
    a2, b2 = e2
    return a1 * a2, a2 * b1 + b2


def _rglru_mixer(x, w_in, conv_w, conv_b, w_a, b_a, w_x, b_x, lam, w_out):
    bsz, s, _ = x.shape
    f32 = jnp.float32
    gate_br, xb = jnp.split(x @ w_in, 2, axis=-1)
    xb = lax.conv_general_dilated(xb, conv_w[:, None, :], window_strides=(1,), padding=[(D_CONV - 1, 0)],
                                  dimension_numbers=('NWC', 'WIO', 'NWC'), feature_group_count=D_RNN) + conv_b
    xg = xb.astype(f32).reshape(bsz, s, D_BLOCKS, D_BLOCK_W)
    r = jax.nn.sigmoid(jnp.einsum('bsnc,ncd->bsnd', xg, w_a.astype(f32)).reshape(bsz, s, D_RNN) + b_a.astype(f32))
    ig = jax.nn.sigmoid(jnp.einsum('bsnc,ncd->bsnd', xg, w_x.astype(f32)).reshape(bsz, s, D_RNN) + b_x.astype(f32))
    log_a = -RG_C * r * jax.nn.softplus(-lam.astype(f32))
    a = jnp.exp(log_a)
    beta = jnp.sqrt(-jnp.expm1(2.0 * log_a)) * (ig * xg.reshape(bsz, s, D_RNN))
    _, hseq = lax.associative_scan(_linrec_combine, (a, beta), axis=1)
    y = hseq * jax.nn.gelu(gate_br.astype(f32))
    return y.astype(x.dtype) @ w_out


def _sqrelu_mlp(x, w_up, w_down):
    return jnp.square(jax.nn.relu(x @ w_up)) @ w_down


def setup_inputs(seed: int = 0) -> dict:
    key = jax.random.key(seed)
    keys = iter(jax.random.split(key, 48))

    def nrm(shape, scale):
        return scale * jax.random.normal(next(keys), shape, jnp.float32)

    def gain(shape):
        return 1.0 + nrm(shape, 0.05)

    n_a, n_b, n_c, n_d = [len(range(m, DEPTH, N_MIXERS)) for m in range(N_MIXERS)]
    d = D_MODEL
    u = jax.random.uniform(next(keys), (n_d, D_RNN), jnp.float32, 0.9, 0.999)
    a0 = u ** (1.0 / RG_C)
    return {
        'x': nrm((BATCH, SEQ, d), 1.0),
        'p': nrm((DEPTH, BATCH, SEQ, PLE_DIM), 1.0),
        'norm_gains': gain((DEPTH, 5, d)),
        'mlp_w_up': nrm((DEPTH, d, D_FF), d ** -0.5),
        'mlp_w_down': nrm((DEPTH, D_FF, d), D_FF ** -0.5),
        'ple_w_up': nrm((DEPTH, PLE_DIM, d), PLE_DIM ** -0.5),
        'ple_w_gate': nrm((DEPTH, d, d), d ** -0.5),
        'a_w_in': nrm((n_a, d, sum(A_SPLITS)), d ** -0.5),
        'a_ig_bias': nrm((n_a, A_HEADS), 0.1),
        'a_fg_bias': jnp.linspace(3.0, 6.0, A_HEADS, dtype=jnp.float32) + nrm((n_a, A_HEADS), 0.1),
        'a_head_gain': gain((n_a, A_HEADS, A_DV)),
        'a_w_out': nrm((n_a, A_HEADS * A_DV, d), (A_HEADS * A_DV) ** -0.5),
        'b_w_in': nrm((n_b, d, sum(B_SPLITS)), d ** -0.5),
        'b_lower_bound': 1.0 + nrm((DEPTH, B_FDIM), 0.1),
        'b_head_gain': gain((n_b, B_HEADS, B_DV)),
        'b_w_out': nrm((n_b, B_HEADS * B_DV, d), (B_HEADS * B_DV) ** -0.5),
        'c_w_in': nrm((n_c, d, 2 * C_HALF), d ** -0.5),
        'c_ln_gain': gain((n_c, C_HALF)),
        'c_ln_bias': nrm((n_c, C_HALF), 0.02),
        'c_spatial_w': nrm((n_c, C_GROUPS, C_CHUNK, C_CHUNK), C_CHUNK ** -0.5),
        'c_spatial_b': 1.0 + nrm((n_c, C_GROUPS, C_CHUNK), 0.02),
        'c_w_out': nrm((n_c, C_HALF, d), C_HALF ** -0.5),
        'd_w_in': nrm((n_d, d, 2 * D_RNN), d ** -0.5),
        'd_conv_w': nrm((n_d, D_CONV, D_RNN), D_CONV ** -0.5),
        'd_conv_b': nrm((n_d, D_RNN), 0.02),
        'd_w_a': nrm((n_d, D_BLOCKS, D_BLOCK_W, D_BLOCK_W), D_BLOCK_W ** -0.5),
        'd_b_a': nrm((n_d, D_RNN), 0.02),
        'd_w_x': nrm((n_d, D_BLOCKS, D_BLOCK_W, D_BLOCK_W), D_BLOCK_W ** -0.5),
        'd_b_x': nrm((n_d, D_RNN), 0.02),
        'd_lambda': jnp.log(a0) - jnp.log1p(-a0),
        'd_w_out': nrm((n_d, D_RNN, d), D_RNN ** -0.5),
    }


def reference(x, p, norm_gains, mlp_w_up, mlp_w_down, ple_w_up, ple_w_gate,
              a_w_in, a_ig_bias, a_fg_bias, a_head_gain, a_w_out,
              b_w_in, b_lower_bound, b_head_gain, b_w_out,
              c_w_in, c_ln_gain, c_ln_bias, c_spatial_w, c_spatial_b, c_w_out,
              d_w_in, d_conv_w, d_conv_b, d_w_a, d_b_a, d_w_x, d_b_x, d_lambda, d_w_out):
    lb = jnp.cumsum(jax.nn.softmax(b_lower_bound.astype(jnp.float32), axis=0), axis=0)
    lb = lb - lb[0]
    h = x
    for i in range(DEPTH):
        kind, j = i % N_MIXERS, i // N_MIXERS
        hn = _rmsnorm(h, norm_gains[i, 0])
        if kind == 0:
            y = _mlstm_mixer(hn, a_w_in[j], a_ig_bias[j], a_fg_bias[j], a_head_gain[j], a_w_out[j])
        elif kind == 1:
            y = _hgrn2_mixer(hn, b_w_in[j], lb[i], b_head_gain[j], b_w_out[j])
        elif kind == 2:
            y = _gmlp_mixer(hn, c_w_in[j], c_ln_gain[j], c_ln_bias[j], c_spatial_w[j], c_spatial_b[j], c_w_out[j])
        else:
            y = _rglru_mixer(hn, d_w_in[j], d_conv_w[j], d_conv_b[j], d_w_a[j], d_b_a[j],
                             d_w_x[j], d_b_x[j], d_lambda[j], d_w_out[j])
        h = h + _rmsnorm(y, norm_gains[i, 1])
        y = _sqrelu_mlp(_rmsnorm(h, norm_gains[i, 2]), mlp_w_up[i], mlp_w_down[i])
        h = h + _rmsnorm(y, norm_gains[i, 3])
        gate = jax.nn.sigmoid(h @ ple_w_gate[i])
        h = h + _rmsnorm(gate * (p[i] @ ple_w_up[i]), norm_gains[i, 4])
    return h
```

```python
import functools

import numpy as np
import jax
import jax.numpy as jnp
from jax import lax
from jax.experimental import pallas as pl
from jax.experimental.pallas import tpu as pltpu

F32 = jnp.float32
MXU_DTYPE = jnp.bfloat16
NORM_EPS = 1e-6
RG_C = 8.0

LANES = 128
CHUNK = 128
VMEM_LIMIT = 56 * 1024 * 1024

A_HEADS, A_DQK, A_DV = 8, 64, 128
B_HEADS, B_DK = 8, 128
C_GROUPS = 8
D_BLOCKS, D_CONV = 4, 4


def _params(*sem):
    return pltpu.CompilerParams(dimension_semantics=sem, vmem_limit_bytes=VMEM_LIMIT)


def _mm(a, b):
    return jnp.dot(a.astype(MXU_DTYPE), b.astype(MXU_DTYPE), preferred_element_type=F32)


def _mm_nt(a, b):
    return lax.dot_general(a.astype(MXU_DTYPE), b.astype(MXU_DTYPE),
                           (((1,), (1,)), ((), ())), preferred_element_type=F32)


def _rms(x, gain):
    return x * lax.rsqrt(jnp.mean(x * x, axis=-1, keepdims=True) + NORM_EPS) * gain


def _sigmoid(x):
    return 1.0 / (1.0 + jnp.exp(-x))


def _log_sigmoid(x):
    return jnp.minimum(x, 0.0) - jnp.log1p(jnp.exp(-jnp.abs(x)))


def _gelu_tanh(x):
    return 0.5 * x * (1.0 + jnp.tanh(0.7978845608028654 * (x + 0.044715 * (x * x * x))))


def _const_spec(shape):
    nd = len(shape)
    return pl.BlockSpec(shape, lambda *_: (0,) * nd)


def _norm_matmul_kernel(h_ref, g_ref, w_ref, o_ref, hn_ref):
    @pl.when(pl.program_id(1) == 0)
    def _():
        hn_ref[...] = _rms(h_ref[...], g_ref[...]).astype(hn_ref.dtype)

    o_ref[...] = jnp.dot(hn_ref[...], w_ref[...], preferred_element_type=F32).astype(o_ref.dtype)


def _norm_matmul(h, gain, w, out_dtype=F32, tm=512, tn=1024):
    m, d = h.shape
    n = w.shape[1]
    tn = min(tn, n)
    return pl.pallas_call(
        _norm_matmul_kernel,
        grid=(m // tm, n // tn),
        in_specs=[pl.BlockSpec((tm, d), lambda i, j: (i, 0)),
                  pl.BlockSpec((1, d), lambda i, j: (0, 0)),
                  pl.BlockSpec((d, tn), lambda i, j: (0, j))],
        out_specs=pl.BlockSpec((tm, tn), lambda i, j: (i, j)),
        out_shape=jax.ShapeDtypeStruct((m, n), out_dtype),
        scratch_shapes=[pltpu.VMEM((tm, d), MXU_DTYPE)],
        compiler_params=_params("parallel", "arbitrary"),
        name="norm_matmul",
    )(h, gain.reshape(1, d), w.astype(MXU_DTYPE))


def _matmul_norm_res_kernel(y_ref, w_ref, g_ref, h_ref, o_ref):
    z = jnp.dot(y_ref[...], w_ref[...], preferred_element_type=F32)
    o_ref[...] = h_ref[...] + _rms(z, g_ref[...])


def _matmul_norm_res(y, w, gain, h, tm=512):
    m, k = y.shape
    d = w.shape[1]
    return pl.pallas_call(
        _matmul_norm_res_kernel,
        grid=(m // tm,),
        in_specs=[pl.BlockSpec((tm, k), lambda i: (i, 0)),
                  _const_spec((k, d)),
                  _const_spec((1, d)),
                  pl.BlockSpec((tm, d), lambda i: (i, 0))],
        out_specs=pl.BlockSpec((tm, d), lambda i: (i, 0)),
        out_shape=jax.ShapeDtypeStruct((m, d), F32),
        compiler_params=_params("parallel"),
        name="matmul_norm_res",
    )(y, w.astype(MXU_DTYPE), gain.reshape(1, d), h)


def _mlp_ple_kernel(h_ref, p_ref, g_ref, wu_ref, wd_ref, wg_ref, wp_ref, o_ref, *, ff_chunk):
    h = h_ref[...]
    hn = _rms(h, g_ref[0:1, :]).astype(MXU_DTYPE)
    d_ff = wu_ref.shape[1]
    acc = jnp.zeros(h.shape, F32)
    for c in range(d_ff // ff_chunk):
        u = jnp.dot(hn, wu_ref[:, c * ff_chunk:(c + 1) * ff_chunk], preferred_element_type=F32)
        u = jnp.maximum(u, 0.0)
        u = (u * u).astype(MXU_DTYPE)
        acc = acc + jnp.dot(u, wd_ref[c * ff_chunk:(c + 1) * ff_chunk, :], preferred_element_type=F32)
    h = h + _rms(acc, g_ref[1:2, :])
    gate = _sigmoid(jnp.dot(h.astype(MXU_DTYPE), wg_ref[...], preferred_element_type=F32))
    emb = jnp.dot(p_ref[...].astype(MXU_DTYPE), wp_ref[...], preferred_element_type=F32)
    o_ref[...] = h + _rms(gate * emb, g_ref[2:3, :])


def _mlp_ple(h, p, gains3, w_up, w_down, w_gate, w_ple, tm=256, ff_chunk=1024):
    m, d = h.shape
    d_ff = w_up.shape[1]
    dp = p.shape[1]
    single = pl.Buffered(1)
    return pl.pallas_call(
        functools.partial(_mlp_ple_kernel, ff_chunk=ff_chunk),
        grid=(m // tm,),
        in_specs=[pl.BlockSpec((tm, d), lambda i: (i, 0)),
                  pl.BlockSpec((tm, dp), lambda i: (i, 0)),
                  _const_spec((3, d)),
                  pl.BlockSpec((d, d_ff), lambda i: (0, 0), pipeline_mode=single),
                  pl.BlockSpec((d_ff, d), lambda i: (0, 0), pipeline_mode=single),
                  pl.BlockSpec((d, d), lambda i: (0, 0), pipeline_mode=single),
                  pl.BlockSpec((dp, d), lambda i: (0, 0), pipeline_mode=single)],
        out_specs=pl.BlockSpec((tm, d), lambda i: (i, 0)),
        out_shape=jax.ShapeDtypeStruct((m, d), F32),
        compiler_params=_params("parallel"),
        name="mlp_ple",
    )(h, p, gains3, w_up.astype(MXU_DTYPE), w_down.astype(MXU_DTYPE),
      w_gate.astype(MXU_DTYPE), w_ple.astype(MXU_DTYPE))


def _tril_mask(n):
    r = lax.broadcasted_iota(jnp.int32, (n, n), 0)
    c = lax.broadcasted_iota(jnp.int32, (n, n), 1)
    return r >= c


def _mlstm_kernel(q_ref, k_ref, v_ref, o_ref, gt_ref, bias_ref, hg_ref, y_ref,
                  c_ref, n_ref, m_ref):
    L = q_ref.shape[0]

    @pl.when(pl.program_id(1) == 0)
    def _():
        c_ref[...] = jnp.zeros(c_ref.shape, F32)
        n_ref[...] = jnp.zeros(n_ref.shape, F32)
        m_ref[...] = jnp.zeros(m_ref.shape, F32)

    causal = _tril_mask(L)
    lane = lax.broadcasted_iota(jnp.int32, (L, LANES), 1)
    pre = gt_ref[...] + bias_ref[...]
    val = jnp.where(lane < A_HEADS, pre, _log_sigmoid(pre))
    bc = jnp.dot(causal.astype(F32), val, preferred_element_type=F32,
                 precision=lax.Precision.HIGHEST)
    val_t = val.T
    bc_t = bc.T
    k_t = k_ref[...].T
    scale = A_DQK ** -0.5

    for hd in range(A_HEADS):
        qh = q_ref[:, hd * A_DQK:(hd + 1) * A_DQK]
        kh = k_ref[:, hd * A_DQK:(hd + 1) * A_DQK]
        kh_t = k_t[hd * A_DQK:(hd + 1) * A_DQK, :]
        vh = v_ref[:, hd * A_DV:(hd + 1) * A_DV]
        i_row = val_t[hd:hd + 1, :]
        i_col = val[:, hd:hd + 1]
        b_row = bc_t[A_HEADS + hd:A_HEADS + hd + 1, :]
        b_col = bc[:, A_HEADS + hd:A_HEADS + hd + 1]
        b_last = bc[L - 1:L, A_HEADS + hd:A_HEADS + hd + 1]
        m_st = m_ref[hd:hd + 1, 0:1]
        c_st = c_ref[hd]
        n_st = n_ref[hd:hd + 1, :]

        inter = b_col + m_st
        dmat = jnp.where(causal, b_col - b_row + i_row, -jnp.inf)
        m_t = jnp.maximum(inter, jnp.max(dmat, axis=1, keepdims=True))
        s = _mm(qh, kh_t) * scale * jnp.exp(dmat - m_t)
        w_inter = jnp.exp(inter - m_t)
        num = _mm(s, vh) + w_inter * (_mm(qh, c_st) * scale)
        den = (jnp.sum(s, axis=1, keepdims=True)
               + w_inter * jnp.sum(qh * (n_st * scale), axis=1, keepdims=True))
        out = num / jnp.maximum(jnp.abs(den), jnp.exp(-m_t))

        tail_row = b_last - b_row + i_row
        tail_col = b_last - b_col + i_col
        m_new = jnp.maximum(b_last + m_st, jnp.max(tail_row, axis=1, keepdims=True))
        decay = jnp.exp(b_last + m_st - m_new)
        c_ref[hd] = decay * c_st + _mm(kh_t * jnp.exp(tail_row - m_new), vh)
        n_ref[hd:hd + 1, :] = decay * n_st + jnp.sum(kh * jnp.exp(tail_col - m_new), axis=0, keepdims=True)
        m_ref[hd:hd + 1, :] = jnp.broadcast_to(m_new, (1, LANES))

        hcell = _rms(out, hg_ref[:, hd * A_DV:(hd + 1) * A_DV])
        gate = _sigmoid(o_ref[:, hd * A_DV:(hd + 1) * A_DV])
        y_ref[:, hd * A_DV:(hd + 1) * A_DV] = (gate * hcell).astype(y_ref.dtype)


def _mlstm_core(z, gates, bias, head_gain, bsz, seq):
    m = z.shape[0]
    d = A_HEADS * A_DV
    nc = seq // CHUNK
    row = lambda b, c: b * nc + c
    qk_w = A_HEADS * A_DQK
    return pl.pallas_call(
        _mlstm_kernel,
        grid=(bsz, nc),
        in_specs=[pl.BlockSpec((CHUNK, qk_w), lambda b, c: (row(b, c), 0)),
                  pl.BlockSpec((CHUNK, qk_w), lambda b, c: (row(b, c), 1)),
                  pl.BlockSpec((CHUNK, d), lambda b, c: (row(b, c), 1)),
                  pl.BlockSpec((CHUNK, d), lambda b, c: (row(b, c), 2)),
                  pl.BlockSpec((CHUNK, LANES), lambda b, c: (row(b, c), 0)),
                  _const_spec((1, LANES)),
                  _const_spec((1, d))],
        out_specs=pl.BlockSpec((CHUNK, d), lambda b, c: (row(b, c), 0)),
        out_shape=jax.ShapeDtypeStruct((m, d), MXU_DTYPE),
        scratch_shapes=[pltpu.VMEM((A_HEADS, A_DQK, A_DV), F32),
                        pltpu.VMEM((A_HEADS, A_DQK), F32),
                        pltpu.VMEM((A_HEADS, LANES), F32)],
        compiler_params=_params("parallel", "arbitrary"),
        name="mlstm_core",
    )(z, z, z, z, gates, bias, head_gain.reshape(1, d))


def _hgrn2_level_matrices(L):
    mats = [np.tril(np.ones((L, L), np.float32))]
    t = np.arange(L)[:, None]
    u = np.arange(L)[None, :]
    w = L // 2
    while w >= 1:
        ref = (t // (2 * w)) * (2 * w) + w - 1
        query = (t % (2 * w)) >= w
        mq = (u > ref) & (u <= t) & query
        mk = (u > t) & (u <= ref) & (~query)
        mats.append((mq | mk).astype(np.float32))
        w //= 2
    return np.stack(mats)


def _split3(x):
    hi = x.astype(MXU_DTYPE)
    r1 = x - hi.astype(F32)
    mid = r1.astype(MXU_DTYPE)
    lo = (r1 - mid.astype(F32)).astype(MXU_DTYPE)
    return jnp.concatenate([hi, mid, lo], axis=1)


def _dot3(mat, x3, n):
    r = jnp.dot(mat, x3, preferred_element_type=F32)
    return r[:, 0:n] + r[:, n:2 * n] + r[:, 2 * n:3 * n]


def _hgrn2_kernel(q_ref, f_ref, i_ref, g_ref, lbp_ref, hg_ref, mats_ref, y_ref, st_ref, *, layer):
    L = q_ref.shape[0]
    dk = B_DK

    @pl.when(pl.program_id(1) == 0)
    def _():
        st_ref[...] = jnp.zeros(st_ref.shape, F32)

    lbp = lbp_ref[...]
    e = jnp.exp(lbp - jnp.max(lbp, axis=0, keepdims=True))
    sm = e / jnp.sum(e, axis=0, keepdims=True)
    cs = sm[0:1, :]
    first = cs
    for r in range(1, layer + 1):
        cs = cs + sm[r:r + 1, :]
    lb = cs - first
    log_lb = jnp.log(lb)
    log_1mlb = jnp.log1p(-lb)

    row = lax.broadcasted_iota(jnp.int32, (L, L), 0)
    col = lax.broadcasted_iota(jnp.int32, (L, L), 1)
    n_levels = mats_ref.shape[0] - 1
    level_masks = []
    w = L // 2
    for _ in range(n_levels):
        level_masks.append(((row // (2 * w)) == (col // (2 * w)))
                           & ((row % (2 * w)) >= w) & ((col % (2 * w)) < w))
        w //= 2
    diag = row == col

    for hd in range(B_HEADS):
        sl = slice(hd * dk, (hd + 1) * dk)
        qh = q_ref[:, sl]
        fz = f_ref[:, sl]
        vh = i_ref[:, sl]
        a = log_lb[:, sl]
        b = log_1mlb[:, sl] + _log_sigmoid(fz)
        log_f = jnp.maximum(a, b) + jnp.log1p(jnp.exp(-jnp.abs(a - b)))
        kh = (1.0 - lb[:, sl]) * _sigmoid(-fz)

        lf3 = _split3(log_f)
        g = _dot3(mats_ref[0], lf3, dk)
        attn = jnp.where(diag, jnp.sum(qh * kh, axis=1, keepdims=True), 0.0)
        for lv in range(n_levels):
            x = jnp.exp(_dot3(mats_ref[lv + 1], lf3, dk))
            attn = jnp.where(level_masks[lv], _mm_nt(qh * x, kh * x), attn)

        st = st_ref[hd]
        g_last = g[L - 1:L, :]
        out = _mm(attn, vh) + _mm_nt(qh * jnp.exp(g), st)
        k_dec = kh * jnp.exp(g_last - g)
        st_ref[hd] = jnp.exp(g_last) * st + _mm(vh.T, k_dec)

        gz = g_ref[:, sl]
        y = _rms(out, hg_ref[:, sl]) * (gz * _sigmoid(gz))
        y_ref[:, sl] = y.astype(y_ref.dtype)


def _hgrn2_core(z, lower_bound, head_gain, layer, bsz, seq):
    m = z.shape[0]
    d = B_HEADS * B_DK
    nc = seq // CHUNK
    row = lambda b, c: b * nc + c
    mats = jnp.asarray(_hgrn2_level_matrices(CHUNK), MXU_DTYPE)
    zspec = lambda j: pl.BlockSpec((CHUNK, d), lambda b, c: (row(b, c), j))
    return pl.pallas_call(
        functools.partial(_hgrn2_kernel, layer=layer),
        grid=(bsz, nc),
        in_specs=[zspec(0), zspec(1), zspec(2), zspec(3),
                  _const_spec(lower_bound.shape),
                  _const_spec((1, d)),
                  _const_spec(mats.shape)],
        out_specs=pl.BlockSpec((CHUNK, d), lambda b, c: (row(b, c), 0)),
        out_shape=jax.ShapeDtypeStruct((m, d), MXU_DTYPE),
        scratch_shapes=[pltpu.VMEM((B_HEADS, B_DK, B_DK), F32)],
        compiler_params=_params("parallel", "arbitrary"),
        name="hgrn2_core",
    )(z, z, z, z, lower_bound, head_gain.reshape(1, d), mats)


def _gmlp_kernel(u_ref, v_ref, lng_ref, lnb_ref, ws_ref, bs_ref, y_ref):
    L = u_ref.shape[0]
    half = u_ref.shape[1]
    gw = half // C_GROUPS
    v = _gelu_tanh(v_ref[...])
    mu = jnp.mean(v, axis=-1, keepdims=True)
    vc = v - mu
    var = jnp.mean(vc * vc, axis=-1, keepdims=True)
    vn = (vc * lax.rsqrt(var + NORM_EPS) * lng_ref[...] + lnb_ref[...]).astype(MXU_DTYPE)
    causal = _tril_mask(L)
    for g in range(C_GROUPS):
        sl = slice(g * gw, (g + 1) * gw)
        w = jnp.where(causal, ws_ref[g], 0.0)
        vm = _mm(w, vn[:, sl]) + bs_ref[:, g:g + 1]
        y_ref[:, sl] = (_gelu_tanh(u_ref[:, sl]) * vm).astype(y_ref.dtype)


def _gmlp_core(z, ln_gain, ln_bias, w_s, b_s):
    m = z.shape[0]
    half = z.shape[1] // 2
    return pl.pallas_call(
        _gmlp_kernel,
        grid=(m // CHUNK,),
        in_specs=[pl.BlockSpec((CHUNK, half), lambda i: (i, 0)),
                  pl.BlockSpec((CHUNK, half), lambda i: (i, 1)),
                  _const_spec((1, half)),
                  _const_spec((1, half)),
                  _const_spec(w_s.shape),
                  _const_spec((CHUNK, C_GROUPS))],
        out_specs=pl.BlockSpec((CHUNK, half), lambda i: (i, 0)),
        out_shape=jax.ShapeDtypeStruct((m, half), MXU_DTYPE),
        compiler_params=_params("parallel"),
        name="gmlp_core",
    )(z, z, ln_gain.reshape(1, half), ln_bias.reshape(1, half), w_s, b_s.T)


def _rglru_kernel(gb_ref, xb_ref, cw_ref, vec_ref, wa_ref, wx_ref, y_ref, tail_ref, hc_ref):
    T, d = xb_ref.shape
    halo = tail_ref.shape[0]

    @pl.when(pl.program_id(1) == 0)
    def _():
        tail_ref[...] = jnp.zeros(tail_ref.shape, F32)
        hc_ref[...] = jnp.zeros(hc_ref.shape, F32)

    xb = xb_ref[...]
    xcat = jnp.concatenate([tail_ref[...], xb], axis=0)
    tail_ref[...] = xb[T - halo:T, :]
    xc = xb * cw_ref[D_CONV - 1:D_CONV, :] + vec_ref[0:1, :]
    for j in range(1, D_CONV):
        shifted = pltpu.roll(xcat, j, 0)[halo:, :]
        xc = xc + shifted * cw_ref[D_CONV - 1 - j:D_CONV - j, :]

    bw = d // D_BLOCKS
    r_parts, i_parts = [], []
    for n in range(D_BLOCKS):
        xn = xc[:, n * bw:(n + 1) * bw]
        r_parts.append(_mm(xn, wa_ref[n]))
        i_parts.append(_mm(xn, wx_ref[n]))
    r = _sigmoid(jnp.concatenate(r_parts, axis=1) + vec_ref[1:2, :])
    ig = _sigmoid(jnp.concatenate(i_parts, axis=1) + vec_ref[2:3, :])
    nlam = -vec_ref[3:4, :]
    softplus = jnp.maximum(nlam, 0.0) + jnp.log1p(jnp.exp(-jnp.abs(nlam)))
    log_a = -RG_C * r * softplus
    a = jnp.exp(log_a)
    b = jnp.sqrt(1.0 - jnp.exp(2.0 * log_a)) * (ig * xc)

    t_idx = lax.broadcasted_iota(jnp.int32, (T, d), 0)
    step = 1
    while step < T:
        keep = t_idx >= step
        a_prev = jnp.where(keep, pltpu.roll(a, step, 0), 1.0)
        b_prev = jnp.where(keep, pltpu.roll(b, step, 0), 0.0)
        b = a * b_prev + b
        a = a * a_prev
        step *= 2
    hseq = b + a * hc_ref[...]
    hc_ref[...] = hseq[T - 1:T, :]
    y_ref[...] = (hseq * _gelu_tanh(gb_ref[...])).astype(y_ref.dtype)


def _rglru_core(z, conv_w, conv_b, w_a, b_a, w_x, b_x, lam, bsz, seq, tt=256):
    m = z.shape[0]
    d = z.shape[1] // 2
    nt = seq // tt
    row = lambda b, t: b * nt + t
    vecs = jnp.stack([conv_b, b_a, b_x, lam])
    return pl.pallas_call(
        _rglru_kernel,
        grid=(bsz, nt),
        in_specs=[pl.BlockSpec((tt, d), lambda b, t: (row(b, t), 0)),
                  pl.BlockSpec((tt, d), lambda b, t: (row(b, t), 1)),
                  _const_spec(conv_w.shape),
                  _const_spec(vecs.shape),
                  _const_spec(w_a.shape),
                  _const_spec(w_x.shape)],
        out_specs=pl.BlockSpec((tt, d), lambda b, t: (row(b, t), 0)),
        out_shape=jax.ShapeDtypeStruct((m, d), MXU_DTYPE),
        scratch_shapes=[pltpu.VMEM((8, d), F32), pltpu.VMEM((1, d), F32)],
        compiler_params=_params("parallel", "arbitrary"),
        name="rglru_core",
    )(z, z, conv_w, vecs, w_a.astype(MXU_DTYPE), w_x.astype(MXU_DTYPE))


def kernel(x, p, norm_gains, mlp_w_up, mlp_w_down, ple_w_up, ple_w_gate, a_w_in, a_ig_bias, a_fg_bias, a_head_gain, a_w_out, b_w_in, b_lower_bound, b_head_gain, b_w_out, c_w_in, c_ln_gain, c_ln_bias, c_spatial_w, c_spatial_b, c_w_out, d_w_in, d_conv_w, d_conv_b, d_w_a, d_b_a, d_w_x, d_b_x, d_lambda, d_w_out):
    bsz, seq, d = x.shape
    depth = norm_gains.shape[0]
    n_mixers = 4
    h = x.reshape(bsz * seq, d)
    for i in range(depth):
        kind, j = i % n_mixers, i // n_mixers
        g = norm_gains[i]
        if kind == 0:
            w_in = a_w_in[j]
            n_main = 2 * A_HEADS * A_DQK + 2 * d
            z = _norm_matmul(h, g[0], w_in[:, :n_main])
            w_gates = jnp.pad(w_in[:, n_main:], ((0, 0), (0, LANES - 2 * A_HEADS)))
            gates = _norm_matmul(h, g[0], w_gates)
            bias = jnp.pad(jnp.concatenate([a_ig_bias[j], a_fg_bias[j]]), (0, LANES - 2 * A_HEADS))
            y = _mlstm_core(z, gates, bias.reshape(1, LANES), a_head_gain[j], bsz, seq)
            w_out = a_w_out[j]
        elif kind == 1:
            z = _norm_matmul(h, g[0], b_w_in[j])
            y = _hgrn2_core(z, b_lower_bound, b_head_gain[j], i, bsz, seq)
            w_out = b_w_out[j]
        elif kind == 2:
            z = _norm_matmul(h, g[0], c_w_in[j])
            y = _gmlp_core(z, c_ln_gain[j], c_ln_bias[j], c_spatial_w[j], c_spatial_b[j])
            w_out = c_w_out[j]
        else:
            z = _norm_matmul(h, g[0], d_w_in[j])
            y = _rglru_core(z, d_conv_w[j], d_conv_b[j], d_w_a[j], d_b_a[j], d_w_x[j], d_b_x[j],
                            d_lambda[j], bsz, seq)
            w_out = d_w_out[j]
        h = _matmul_norm_res(y, w_out, g[1], h)
        h = _mlp_ple(h, p[i].reshape(bsz * seq, -1), g[2:5], mlp_w_up[i], mlp_w_down[i],
                     ple_w_gate[i], ple_w_up[i])
    return h.reshape(bsz, seq, d)
```

```python
import functools

import numpy as np
import jax
import jax.numpy as jnp
from jax import lax
from jax.experimental import pallas as pl
from jax.experimental.pallas import tpu as pltpu

F32 = jnp.float32
MXU_DTYPE = jnp.bfloat16
NORM_EPS = 1e-6
RG_C = 8.0

LANES = 128
CHUNK = 128
VMEM_LIMIT = 56 * 1024 * 1024

A_HEADS, A_DQK, A_DV = 8, 64, 128
B_HEADS, B_DK = 8, 128
C_GROUPS = 8
D_BLOCKS, D_CONV = 4, 4


def _params(*sem):
    return pltpu.CompilerParams(dimension_semantics=sem, vmem_limit_bytes=VMEM_LIMIT)


def _mm(a, b):
    return jnp.dot(a.astype(MXU_DTYPE), b.astype(MXU_DTYPE), preferred_element_type=F32)


def _mm_nt(a, b):
    return lax.dot_general(a.astype(MXU_DTYPE), b.astype(MXU_DTYPE),
                           (((1,), (1,)), ((), ())), preferred_element_type=F32)


def _rms(x, gain):
    return x * lax.rsqrt(jnp.mean(x * x, axis=-1, keepdims=True) + NORM_EPS) * gain


def _sigmoid(x):
    return 1.0 / (1.0 + jnp.exp(-x))


def _log_sigmoid(x):
    return jnp.minimum(x, 0.0) - jnp.log1p(jnp.exp(-jnp.abs(x)))


def _gelu_tanh(x):
    return 0.5 * x * (1.0 + jnp.tanh(0.7978845608028654 * (x + 0.044715 * (x * x * x))))


def _const_spec(shape):
    nd = len(shape)
    return pl.BlockSpec(shape, lambda *_: (0,) * nd)


def _single_buffered(shape):
    nd = len(shape)
    return pl.BlockSpec(shape, lambda *_: (0,) * nd, pipeline_mode=pl.Buffered(1))


def _norm_matmul_kernel(h_ref, g_ref, w_ref, *o_refs, col_chunk):
    hn = _rms(h_ref[...], g_ref[...]).astype(MXU_DTYPE)
    off = 0
    for o_ref in o_refs:
        n = o_ref.shape[1]
        for c in range(0, n, col_chunk):
            cw = min(col_chunk, n - c)
            o_ref[:, c:c + cw] = jnp.dot(hn, w_ref[:, off + c:off + c + cw],
                                         preferred_element_type=F32).astype(o_ref.dtype)
        off += n


def _norm_matmul(h, gain, w, groups, tm=512, col_chunk=1024):
    m, d = h.shape
    assert sum(n for n, _ in groups) == w.shape[1]
    return pl.pallas_call(
        functools.partial(_norm_matmul_kernel, col_chunk=col_chunk),
        grid=(m // tm,),
        in_specs=[pl.BlockSpec((tm, d), lambda i: (i, 0)),
                  _const_spec((1, d)),
                  _single_buffered(w.shape)],
        out_specs=[pl.BlockSpec((tm, n), lambda i: (i, 0)) for n, _ in groups],
        out_shape=[jax.ShapeDtypeStruct((m, n), dt) for n, dt in groups],
        compiler_params=_params("parallel"),
        name="norm_matmul",
    )(h, gain.reshape(1, d), w.astype(MXU_DTYPE))


def _post_mixer_kernel(h_ref, y_ref, p_ref, g_ref, wo_ref, wu_ref, wd_ref, wg_ref, wp_ref, o_ref,
                       *, ff_chunk):
    h = h_ref[...] + _rms(jnp.dot(y_ref[...], wo_ref[...], preferred_element_type=F32), g_ref[0:1, :])
    hn = _rms(h, g_ref[1:2, :]).astype(MXU_DTYPE)
    d_ff = wu_ref.shape[1]
    acc = jnp.zeros(h.shape, F32)
    for c in range(d_ff // ff_chunk):
        u = jnp.dot(hn, wu_ref[:, c * ff_chunk:(c + 1) * ff_chunk], preferred_element_type=F32)
        u = jnp.maximum(u, 0.0)
        u = (u * u).astype(MXU_DTYPE)
        acc = acc + jnp.dot(u, wd_ref[c * ff_chunk:(c + 1) * ff_chunk, :], preferred_element_type=F32)
    h = h + _rms(acc, g_ref[2:3, :])
    gate = _sigmoid(jnp.dot(h.astype(MXU_DTYPE), wg_ref[...], preferred_element_type=F32))
    emb = jnp.dot(p_ref[...].astype(MXU_DTYPE), wp_ref[...], preferred_element_type=F32)
    o_ref[...] = h + _rms(gate * emb, g_ref[3:4, :])


def _post_mixer(h, y, p, gains4, w_out, w_up, w_down, w_gate, w_ple, tm=256, ff_chunk=1024):
    m, d = h.shape
    k = y.shape[1]
    dp = p.shape[1]
    weights = [w.astype(MXU_DTYPE) for w in (w_out, w_up, w_down, w_gate, w_ple)]
    return pl.pallas_call(
        functools.partial(_post_mixer_kernel, ff_chunk=ff_chunk),
        grid=(m // tm,),
        in_specs=[pl.BlockSpec((tm, d), lambda i: (i, 0)),
                  pl.BlockSpec((tm, k), lambda i: (i, 0)),
                  pl.BlockSpec((tm, dp), lambda i: (i, 0)),
                  _const_spec((4, d))] + [_single_buffered(w.shape) for w in weights],
        out_specs=pl.BlockSpec((tm, d), lambda i: (i, 0)),
        out_shape=jax.ShapeDtypeStruct((m, d), F32),
        compiler_params=_params("parallel"),
        name="post_mixer",
    )(h, y, p, gains4, *weights)


def _tril_mask(n):
    r = lax.broadcasted_iota(jnp.int32, (n, n), 0)
    c = lax.broadcasted_iota(jnp.int32, (n, n), 1)
    return r >= c


def _mlstm_kernel(q_ref, k_ref, v_ref, o_ref, gt_ref, bias_ref, hg_ref, y_ref, c_ref, m_ref):
    L = q_ref.shape[0]

    @pl.when(pl.program_id(1) == 0)
    def _():
        c_ref[...] = jnp.zeros(c_ref.shape, F32)
        m_ref[...] = jnp.zeros(m_ref.shape, F32)

    causal = _tril_mask(L)
    lane = lax.broadcasted_iota(jnp.int32, (L, LANES), 1)
    t_idx = lax.broadcasted_iota(jnp.int32, (L, LANES), 0)
    pre = gt_ref[...] + bias_ref[...]
    val = jnp.where(lane < A_HEADS, pre, jnp.where(lane < 2 * A_HEADS, _log_sigmoid(pre), 0.0))
    bc = jnp.dot(causal.astype(F32), val, preferred_element_type=F32,
                 precision=lax.Precision.HIGHEST)
    b = pltpu.roll(bc, LANES - A_HEADS, 1)
    a = val - b
    cm = a
    step = 1
    while step < L:
        cm = jnp.maximum(cm, jnp.where(t_idx >= step, pltpu.roll(cm, step, 0), -jnp.inf))
        step *= 2
    m_st = m_ref[...]
    mx = jnp.maximum(cm, m_st)
    mx_last = mx[L - 1:L, :]
    m_ref[...] = b[L - 1:L, :] + mx_last
    scale = A_DQK ** -0.5
    decay = jnp.exp(m_st - mx_last)
    w_inter = jnp.exp(m_st - mx) * scale
    e_negm = jnp.exp(-(b + mx))
    a_t = a.T
    wk_t = jnp.exp(a - mx_last).T
    k_t = k_ref[...].astype(F32).T
    ones_col = (lax.broadcasted_iota(jnp.int32, (L, A_DV), 1) == 0).astype(MXU_DTYPE)

    for hd in range(A_HEADS):
        qh = q_ref[:, hd * A_DQK:(hd + 1) * A_DQK].astype(MXU_DTYPE)
        kh_t = k_t[hd * A_DQK:(hd + 1) * A_DQK, :]
        vh = v_ref[:, hd * A_DV:(hd + 1) * A_DV].astype(MXU_DTYPE)
        c_st = c_ref[hd]

        dmat = jnp.exp(jnp.where(causal, a_t[hd:hd + 1, :] - mx[:, hd:hd + 1], -jnp.inf))
        s = _mm(qh, kh_t) * scale * dmat
        inter = _mm(qh, c_st)
        wi = w_inter[:, hd:hd + 1]
        num = _mm(s, vh) + wi * inter[:, :A_DV]
        den = jnp.sum(s, axis=1, keepdims=True) + wi * inter[:, A_DV:A_DV + 1]
        out = num * (1.0 / jnp.maximum(jnp.abs(den), e_negm[:, hd:hd + 1]))

        v_ext = jnp.concatenate([vh, ones_col], axis=1)
        c_ref[hd] = decay[:, hd:hd + 1] * c_st + _mm(kh_t * wk_t[hd:hd + 1, :], v_ext)

        hcell = _rms(out, hg_ref[:, hd * A_DV:(hd + 1) * A_DV])
        gate = _sigmoid(o_ref[:, hd * A_DV:(hd + 1) * A_DV])
        y_ref[:, hd * A_DV:(hd + 1) * A_DV] = (gate * hcell).astype(y_ref.dtype)


def _mlstm_core(qk, v, o, gates, bias, head_gain, bsz, seq):
    m = qk.shape[0]
    d = A_HEADS * A_DV
    nc = seq // CHUNK
    row = lambda b, c: b * nc + c
    qk_w = A_HEADS * A_DQK
    return pl.pallas_call(
        _mlstm_kernel,
        grid=(bsz, nc),
        in_specs=[pl.BlockSpec((CHUNK, qk_w), lambda b, c: (row(b, c), 0)),
                  pl.BlockSpec((CHUNK, qk_w), lambda b, c: (row(b, c), 1)),
                  pl.BlockSpec((CHUNK, d), lambda b, c: (row(b, c), 0)),
                  pl.BlockSpec((CHUNK, d), lambda b, c: (row(b, c), 0)),
                  pl.BlockSpec((CHUNK, LANES), lambda b, c: (row(b, c), 0)),
                  _const_spec((1, LANES)),
                  _const_spec((1, d))],
        out_specs=pl.BlockSpec((CHUNK, d), lambda b, c: (row(b, c), 0)),
        out_shape=jax.ShapeDtypeStruct((m, d), MXU_DTYPE),
        scratch_shapes=[pltpu.VMEM((A_HEADS, A_DQK, 2 * A_DV), F32),
                        pltpu.VMEM((1, LANES), F32)],
        compiler_params=_params("parallel", "arbitrary"),
        name="mlstm_core",
    )(qk, qk, v, o, gates, bias, head_gain.reshape(1, d))


SUBLANES = 8


def _block_ref_rows(g, w):
    L, D = g.shape
    if 2 * w >= SUBLANES:
        g3 = g.reshape(L // (2 * w), 2 * w, D)
        return jnp.broadcast_to(g3[:, w - 1:w, :], g3.shape).reshape(L, D)
    g3 = g.reshape(L // SUBLANES, SUBLANES, D)
    sub = lax.broadcasted_iota(jnp.int32, g3.shape, 1)
    out = None
    for start in range(0, SUBLANES, 2 * w):
        r = jnp.broadcast_to(g3[:, start + w - 1:start + w, :], g3.shape)
        out = r if out is None else jnp.where(sub >= start, r, out)
    return out.reshape(L, D)


def _split3(x):
    hi = x.astype(MXU_DTYPE)
    r1 = x - hi.astype(F32)
    mid = r1.astype(MXU_DTYPE)
    lo = (r1 - mid.astype(F32)).astype(MXU_DTYPE)
    return jnp.concatenate([hi, mid, lo], axis=1)


def _dot3(mat, x3, n):
    r = jnp.dot(mat, x3, preferred_element_type=F32)
    return r[:, 0:n] + r[:, n:2 * n] + r[:, 2 * n:3 * n]


def _hgrn2_kernel(q_ref, f_ref, i_ref, g_ref, lbp_ref, hg_ref, y_ref, st_ref, *, layer):
    L, D = q_ref.shape
    dk = B_DK

    @pl.when(pl.program_id(1) == 0)
    def _():
        st_ref[...] = jnp.zeros(st_ref.shape, F32)

    lbp = lbp_ref[...]
    e = jnp.exp(lbp - jnp.max(lbp, axis=0, keepdims=True))
    sm = e / jnp.sum(e, axis=0, keepdims=True)
    cs = sm[0:1, :]
    first = cs
    for r in range(1, layer + 1):
        cs = cs + sm[r:r + 1, :]
    lb = cs - first
    log_lb = jnp.log(lb)
    log_1mlb = jnp.log1p(-lb)

    row = lax.broadcasted_iota(jnp.int32, (L, L), 0)
    col = lax.broadcasted_iota(jnp.int32, (L, L), 1)
    trow = lax.broadcasted_iota(jnp.int32, (L, 1), 0)
    diag = row == col

    q = q_ref[...]
    fz = f_ref[...]
    a = log_lb
    b = log_1mlb + _log_sigmoid(fz)
    log_f = jnp.maximum(a, b) + jnp.log1p(jnp.exp(-jnp.abs(a - b)))
    k = (1.0 - lb) * _sigmoid(-fz)
    g = _dot3((row >= col).astype(MXU_DTYPE), _split3(log_f), D)

    attn = []
    for hd in range(B_HEADS):
        sl = slice(hd * dk, (hd + 1) * dk)
        attn.append(jnp.where(diag, jnp.sum(q[:, sl] * k[:, sl], axis=1, keepdims=True), 0.0))
    w = L // 2
    while w >= 1:
        upper = (trow % (2 * w)) >= w
        ref = _block_ref_rows(g, w)
        x = jnp.exp(jnp.where(upper, g - ref, ref - g))
        z = (jnp.where(upper, q, k) * x).astype(MXU_DTYPE)
        mask = ((row // (2 * w)) == (col // (2 * w))) & ((row % (2 * w)) >= w) & ((col % (2 * w)) < w)
        for hd in range(B_HEADS):
            zh = z[:, hd * dk:(hd + 1) * dk]
            attn[hd] = jnp.where(mask, _mm_nt(zh, zh), attn[hd])
        w //= 2

    g_last = g[L - 1:L, :]
    q_dec = (q * jnp.exp(g)).astype(MXU_DTYPE)
    k_dec = (k * jnp.exp(g_last - g)).astype(MXU_DTYPE)
    s_dec = jnp.exp(g_last)
    for hd in range(B_HEADS):
        sl = slice(hd * dk, (hd + 1) * dk)
        vh = i_ref[:, sl]
        st = st_ref[hd]
        out = _mm(attn[hd], vh) + _mm_nt(q_dec[:, sl], st)
        st_ref[hd] = s_dec[:, sl] * st + _mm(vh.astype(F32).T, k_dec[:, sl])
        gz = g_ref[:, sl]
        y = _rms(out, hg_ref[:, sl]) * (gz * _sigmoid(gz))
        y_ref[:, sl] = y.astype(y_ref.dtype)


def _hgrn2_core(qf, iv, gz, lower_bound, head_gain, layer, bsz, seq):
    m = qf.shape[0]
    d = B_HEADS * B_DK
    nc = seq // CHUNK
    row = lambda b, c: b * nc + c
    zspec = lambda j: pl.BlockSpec((CHUNK, d), lambda b, c: (row(b, c), j))
    return pl.pallas_call(
        functools.partial(_hgrn2_kernel, layer=layer),
        grid=(bsz, nc),
        in_specs=[zspec(0), zspec(1), zspec(0), zspec(0),
                  _const_spec(lower_bound.shape),
                  _const_spec((1, d))],
        out_specs=pl.BlockSpec((CHUNK, d), lambda b, c: (row(b, c), 0)),
        out_shape=jax.ShapeDtypeStruct((m, d), MXU_DTYPE),
        scratch_shapes=[pltpu.VMEM((B_HEADS, B_DK, B_DK), F32)],
        compiler_params=_params("parallel", "arbitrary"),
        name="hgrn2_core",
    )(qf, qf, iv, gz, lower_bound, head_gain.reshape(1, d))


def _gmlp_kernel(u_ref, v_ref, lng_ref, lnb_ref, ws_ref, bs_ref, y_ref):
    L = u_ref.shape[0]
    half = u_ref.shape[1]
    gw = half // C_GROUPS
    v = _gelu_tanh(v_ref[...])
    mu = jnp.mean(v, axis=-1, keepdims=True)
    vc = v - mu
    var = jnp.mean(vc * vc, axis=-1, keepdims=True)
    vn = (vc * lax.rsqrt(var + NORM_EPS) * lng_ref[...] + lnb_ref[...]).astype(MXU_DTYPE)
    causal = _tril_mask(L)
    for g in range(C_GROUPS):
        sl = slice(g * gw, (g + 1) * gw)
        w = jnp.where(causal, ws_ref[g], 0.0)
        vm = _mm(w, vn[:, sl]) + bs_ref[:, g:g + 1]
        y_ref[:, sl] = (_gelu_tanh(u_ref[:, sl]) * vm).astype(y_ref.dtype)


def _gmlp_core(z, ln_gain, ln_bias, w_s, b_s):
    m = z.shape[0]
    half = z.shape[1] // 2
    return pl.pallas_call(
        _gmlp_kernel,
        grid=(m // CHUNK,),
        in_specs=[pl.BlockSpec((CHUNK, half), lambda i: (i, 0)),
                  pl.BlockSpec((CHUNK, half), lambda i: (i, 1)),
                  _const_spec((1, half)),
                  _const_spec((1, half)),
                  _const_spec(w_s.shape),
                  _const_spec((CHUNK, C_GROUPS))],
        out_specs=pl.BlockSpec((CHUNK, half), lambda i: (i, 0)),
        out_shape=jax.ShapeDtypeStruct((m, half), MXU_DTYPE),
        compiler_params=_params("parallel"),
        name="gmlp_core",
    )(z, z, ln_gain.reshape(1, half), ln_bias.reshape(1, half), w_s, b_s.T)


def _rglru_kernel(gb_ref, xb_ref, cw_ref, vec_ref, wa_ref, wx_ref, y_ref, tail_ref, hc_ref):
    T, d = xb_ref.shape
    halo = tail_ref.shape[0]

    @pl.when(pl.program_id(1) == 0)
    def _():
        tail_ref[...] = jnp.zeros(tail_ref.shape, F32)
        hc_ref[...] = jnp.zeros(hc_ref.shape, F32)

    xb = xb_ref[...]
    xcat = jnp.concatenate([tail_ref[...], xb], axis=0)
    tail_ref[...] = xb[T - halo:T, :]
    xc = xb * cw_ref[D_CONV - 1:D_CONV, :] + vec_ref[0:1, :]
    for j in range(1, D_CONV):
        shifted = pltpu.roll(xcat, j, 0)[halo:, :]
        xc = xc + shifted * cw_ref[D_CONV - 1 - j:D_CONV - j, :]

    bw = d // D_BLOCKS
    r_parts, i_parts = [], []
    for n in range(D_BLOCKS):
        xn = xc[:, n * bw:(n + 1) * bw]
        r_parts.append(_mm(xn, wa_ref[n]))
        i_parts.append(_mm(xn, wx_ref[n]))
    r = _sigmoid(jnp.concatenate(r_parts, axis=1) + vec_ref[1:2, :])
    ig = _sigmoid(jnp.concatenate(i_parts, axis=1) + vec_ref[2:3, :])
    nlam = -vec_ref[3:4, :]
    softplus = jnp.maximum(nlam, 0.0) + jnp.log1p(jnp.exp(-jnp.abs(nlam)))
    log_a = -RG_C * r * softplus
    a = jnp.exp(log_a)
    b = jnp.sqrt(1.0 - jnp.exp(2.0 * log_a)) * (ig * xc)

    t_idx = lax.broadcasted_iota(jnp.int32, (T, d), 0)
    step = 1
    while step < T:
        keep = t_idx >= step
        a_prev = jnp.where(keep, pltpu.roll(a, step, 0), 1.0)
        b_prev = jnp.where(keep, pltpu.roll(b, step, 0), 0.0)
        b = a * b_prev + b
        a = a * a_prev
        step *= 2
    hseq = b + a * hc_ref[...]
    hc_ref[...] = hseq[T - 1:T, :]
    y_ref[...] = (hseq * _gelu_tanh(gb_ref[...])).astype(y_ref.dtype)


def _rglru_core(z, conv_w, conv_b, w_a, b_a, w_x, b_x, lam, bsz, seq, tt=256):
    m = z.shape[0]
    d = z.shape[1] // 2
    nt = seq // tt
    row = lambda b, t: b * nt + t
    vecs = jnp.stack([conv_b, b_a, b_x, lam])
    return pl.pallas_call(
        _rglru_kernel,
        grid=(bsz, nt),
        in_specs=[pl.BlockSpec((tt, d), lambda b, t: (row(b, t), 0)),
                  pl.BlockSpec((tt, d), lambda b, t: (row(b, t), 1)),
                  _const_spec(conv_w.shape),
                  _const_spec(vecs.shape),
                  _const_spec(w_a.shape),
                  _const_spec(w_x.shape)],
        out_specs=pl.BlockSpec((tt, d), lambda b, t: (row(b, t), 0)),
        out_shape=jax.ShapeDtypeStruct((m, d), MXU_DTYPE),
        scratch_shapes=[pltpu.VMEM((8, d), F32), pltpu.VMEM((1, d), F32)],
        compiler_params=_params("parallel", "arbitrary"),
        name="rglru_core",
    )(z, z, conv_w, vecs, w_a.astype(MXU_DTYPE), w_x.astype(MXU_DTYPE))


def kernel(x, p, norm_gains, mlp_w_up, mlp_w_down, ple_w_up, ple_w_gate, a_w_in, a_ig_bias, a_fg_bias, a_head_gain, a_w_out, b_w_in, b_lower_bound, b_head_gain, b_w_out, c_w_in, c_ln_gain, c_ln_bias, c_spatial_w, c_spatial_b, c_w_out, d_w_in, d_conv_w, d_conv_b, d_w_a, d_b_a, d_w_x, d_b_x, d_lambda, d_w_out):
    bsz, seq, d = x.shape
    depth = norm_gains.shape[0]
    n_mixers = 4
    h = x.reshape(bsz * seq, d)
    for i in range(depth):
        kind, j = i % n_mixers, i // n_mixers
        g = norm_gains[i]
        if kind == 0:
            w_in = jnp.pad(a_w_in[j], ((0, 0), (0, LANES - 2 * A_HEADS)))
            qk, v, o, gates = _norm_matmul(
                h, g[0], w_in,
                [(2 * A_HEADS * A_DQK, MXU_DTYPE), (d, MXU_DTYPE), (d, F32), (LANES, F32)])
            bias = jnp.pad(jnp.concatenate([a_ig_bias[j], a_fg_bias[j]]), (0, LANES - 2 * A_HEADS))
            y = _mlstm_core(qk, v, o, gates, bias.reshape(1, LANES), a_head_gain[j], bsz, seq)
            w_out = a_w_out[j]
        elif kind == 1:
            qf, iv, gz = _norm_matmul(h, g[0], b_w_in[j], [(2 * d, F32), (d, MXU_DTYPE), (d, F32)])
            y = _hgrn2_core(qf, iv, gz, b_lower_bound, b_head_gain[j], i, bsz, seq)
            w_out = b_w_out[j]
        elif kind == 2:
            z, = _norm_matmul(h, g[0], c_w_in[j], [(c_w_in.shape[2], F32)])
            y = _gmlp_core(z, c_ln_gain[j], c_ln_bias[j], c_spatial_w[j], c_spatial_b[j])
            w_out = c_w_out[j]
        else:
            z, = _norm_matmul(h, g[0], d_w_in[j], [(d_w_in.shape[2], F32)])
            y = _rglru_core(z, d_conv_w[j], d_conv_b[j], d_w_a[j], d_b_a[j], d_w_x[j], d_b_x[j],
                            d_lambda[j], bsz, seq)
            w_out = d_w_out[j]
        h = _post_mixer(h, y, p[i].reshape(bsz * seq, -1), g[1:5], w_out, mlp_w_up[i], mlp_w_down[i],
                        ple_w_gate[i], ple_w_up[i])
    return h.reshape(bsz, seq, d)
```

```python
import functools

import numpy as np
import jax
import jax.numpy as jnp
from jax import lax
from jax.experimental import pallas as pl
from jax.experimental.pallas import tpu as pltpu

F32 = jnp.float32
MXU_DTYPE = jnp.bfloat16
NORM_EPS = 1e-6
RG_C = 8.0

LANES = 128
SUBLANES = 8
SQRT_FLOOR = 1e-30
CHUNK = 128
VMEM_LIMIT = 56 * 1024 * 1024

A_HEADS, A_DQK, A_DV = 8, 64, 128
B_HEADS, B_DK = 8, 128
C_GROUPS = 8
D_BLOCKS, D_CONV = 4, 4


def _params(*sem):
    return pltpu.CompilerParams(dimension_semantics=sem, vmem_limit_bytes=VMEM_LIMIT)


def _mm(a, b):
    return jnp.dot(a.astype(MXU_DTYPE), b.astype(MXU_DTYPE), preferred_element_type=F32)


def _mm_nt(a, b):
    return lax.dot_general(a.astype(MXU_DTYPE), b.astype(MXU_DTYPE),
                           (((1,), (1,)), ((), ())), preferred_element_type=F32)


def _rms(x, gain):
    return x * lax.rsqrt(jnp.mean(x * x, axis=-1, keepdims=True) + NORM_EPS) * gain


def _sigmoid(x):
    return 0.5 * jnp.tanh(0.5 * x) + 0.5


def _log_sigmoid(x):
    return jnp.minimum(x, 0.0) - jnp.log(1.0 + jnp.exp(-jnp.abs(x)))


_GELU_K = 0.7978845608028654


def _gelu_tanh(x):
    hx = 0.5 * x
    return hx + hx * jnp.tanh(x * (_GELU_K + (_GELU_K * 0.044715) * (x * x)))


def _ep_gelu(z, aux_ref, c, cw):
    return (_gelu_tanh(z),)


def _ep_sigmoid(z, aux_ref, c, cw):
    return (_sigmoid(z),)


def _ep_silu(z, aux_ref, c, cw):
    return (z * _sigmoid(z),)


def _hgrn2_gate_epilogue(fz, lbp_ref, c, cw, *, layer):
    lbp = lbp_ref[:, c:c + cw]
    e = jnp.exp(lbp - jnp.max(lbp, axis=0, keepdims=True))
    sm = e / jnp.sum(e, axis=0, keepdims=True)
    cs = sm[0:1, :]
    first = cs
    for r in range(1, layer + 1):
        cs = cs + sm[r:r + 1, :]
    lb = cs - first
    t = jnp.exp(-jnp.abs(fz))
    r = 1.0 / (1.0 + t)
    sig_abs, sig_nabs = r, t * r
    pos = fz >= 0.0
    sig = jnp.where(pos, sig_abs, sig_nabs)
    sig_neg = jnp.where(pos, sig_nabs, sig_abs)
    return jnp.log(lb + (1.0 - lb) * sig), (1.0 - lb) * sig_neg


def _const_spec(shape):
    nd = len(shape)
    return pl.BlockSpec(shape, lambda *_: (0,) * nd)


def _single_buffered(shape):
    nd = len(shape)
    return pl.BlockSpec(shape, lambda *_: (0,) * nd, pipeline_mode=pl.Buffered(1))


def _norm_matmul_kernel(h_ref, g_ref, w_ref, aux_ref, *o_refs, plan, col_chunk):
    hn = _rms(h_ref[...], g_ref[...]).astype(MXU_DTYPE)
    off = 0
    oi = 0
    for n, epilogue, dtypes in plan:
        outs = o_refs[oi:oi + len(dtypes)]
        for c in range(0, n, col_chunk):
            cw = min(col_chunk, n - c)
            z = jnp.dot(hn, w_ref[:, off + c:off + c + cw], preferred_element_type=F32)
            vals = (z,) if epilogue is None else epilogue(z, aux_ref, c, cw)
            for o_ref, val in zip(outs, vals):
                o_ref[:, c:c + cw] = val.astype(o_ref.dtype)
        off += n
        oi += len(dtypes)


def _norm_matmul(h, gain, w, plan, aux=None, tm=512, col_chunk=1024):
    m, d = h.shape
    assert sum(n for n, _, _ in plan) == w.shape[1]
    if aux is None:
        aux = jnp.zeros((1, d), F32)
    outs = [(n, dt) for n, _, dts in plan for dt in dts]
    return pl.pallas_call(
        functools.partial(_norm_matmul_kernel, plan=plan, col_chunk=col_chunk),
        grid=(m // tm,),
        in_specs=[pl.BlockSpec((tm, d), lambda i: (i, 0)),
                  _const_spec((1, d)),
                  _single_buffered(w.shape),
                  _const_spec(aux.shape)],
        out_specs=[pl.BlockSpec((tm, n), lambda i: (i, 0)) for n, _ in outs],
        out_shape=[jax.ShapeDtypeStruct((m, n), dt) for n, dt in outs],
        compiler_params=_params("parallel"),
        name="norm_matmul",
    )(h, gain.reshape(1, d), w.astype(MXU_DTYPE), aux)


def _post_mixer_kernel(h_ref, y_ref, p_ref, g_ref, wo_ref, wu_ref, wd_ref, wg_ref, wp_ref, o_ref,
                       *, ff_chunk, sub_rows):
    n_ff = wu_ref.shape[1] // ff_chunk
    n_sub = h_ref.shape[0] // sub_rows
    rows = [slice(r * sub_rows, (r + 1) * sub_rows) for r in range(n_sub)]

    def head(r):
        h = h_ref[rows[r], :] + _rms(jnp.dot(y_ref[rows[r], :], wo_ref[...], preferred_element_type=F32),
                                     g_ref[0:1, :])
        return h, _rms(h, g_ref[1:2, :]).astype(MXU_DTYPE), jnp.zeros(h.shape, F32)

    def ff(hn, acc, c):
        u = jnp.dot(hn, wu_ref[:, c * ff_chunk:(c + 1) * ff_chunk], preferred_element_type=F32)
        u = jnp.maximum(u, 0.0)
        u = (u * u).astype(MXU_DTYPE)
        return acc + jnp.dot(u, wd_ref[c * ff_chunk:(c + 1) * ff_chunk, :], preferred_element_type=F32)

    def tail(r, h, acc):
        h = h + _rms(acc, g_ref[2:3, :])
        gate = _sigmoid(jnp.dot(h.astype(MXU_DTYPE), wg_ref[...], preferred_element_type=F32))
        emb = jnp.dot(p_ref[rows[r], :].astype(MXU_DTYPE), wp_ref[...], preferred_element_type=F32)
        o_ref[rows[r], :] = h + _rms(gate * emb, g_ref[3:4, :])

    half = n_ff // 2
    prev = None
    for r in range(n_sub):
        h, hn, acc = head(r)
        if prev is not None:
            ph, phn, pacc = prev
            for c in range(half, n_ff):
                pacc = ff(phn, pacc, c)
        for c in range(half):
            acc = ff(hn, acc, c)
        if prev is not None:
            tail(r - 1, ph, pacc)
        prev = (h, hn, acc)
    ph, phn, pacc = prev
    for c in range(half, n_ff):
        pacc = ff(phn, pacc, c)
    tail(n_sub - 1, ph, pacc)


def _post_mixer(h, y, p, gains4, w_out, w_up, w_down, w_gate, w_ple, tm=512, sub_rows=256, ff_chunk=1024):
    m, d = h.shape
    k = y.shape[1]
    dp = p.shape[1]
    weights = [w.astype(MXU_DTYPE) for w in (w_out, w_up, w_down, w_gate, w_ple)]
    return pl.pallas_call(
        functools.partial(_post_mixer_kernel, ff_chunk=ff_chunk, sub_rows=sub_rows),
        grid=(m // tm,),
        in_specs=[pl.BlockSpec((tm, d), lambda i: (i, 0)),
                  pl.BlockSpec((tm, k), lambda i: (i, 0)),
                  pl.BlockSpec((tm, dp), lambda i: (i, 0)),
                  _const_spec((4, d))] + [_single_buffered(w.shape) for w in weights],
        out_specs=pl.BlockSpec((tm, d), lambda i: (i, 0)),
        out_shape=jax.ShapeDtypeStruct((m, d), F32),
        compiler_params=_params("parallel"),
        name="post_mixer",
    )(h, y, p, gains4, *weights)


def _tril_mask(n):
    r = lax.broadcasted_iota(jnp.int32, (n, n), 0)
    c = lax.broadcasted_iota(jnp.int32, (n, n), 1)
    return r >= c


def _mlstm_kernel(q_ref, k_ref, v_ref, o_ref, gt_ref, bias_ref, hg_ref, y_ref, c_ref, m_ref):
    @pl.when(pl.program_id(1) == 0)
    def _():
        c_ref[...] = jnp.zeros(c_ref.shape, F32)
        m_ref[...] = jnp.zeros(m_ref.shape, F32)

    for r in range(q_ref.shape[0]):
        _mlstm_chunk(q_ref.at[r], k_ref.at[r], v_ref.at[r], o_ref.at[r], gt_ref.at[r], bias_ref, hg_ref,
                     y_ref.at[r], c_ref.at[r], m_ref.at[r])


def _mlstm_chunk(q_ref, k_ref, v_ref, o_ref, gt_ref, bias_ref, hg_ref, y_ref, c_ref, m_ref):
    L = q_ref.shape[0]
    causal = _tril_mask(L)
    lane = lax.broadcasted_iota(jnp.int32, (L, LANES), 1)
    t_idx = lax.broadcasted_iota(jnp.int32, (L, LANES), 0)
    pre = gt_ref[...] + bias_ref[...]
    val = jnp.where(lane < A_HEADS, pre, jnp.where(lane < 2 * A_HEADS, _log_sigmoid(pre), 0.0))
    bc = jnp.dot(causal.astype(F32), val, preferred_element_type=F32,
                 precision=lax.Precision.HIGHEST)
    b = pltpu.roll(bc, LANES - A_HEADS, 1)
    a = val - b
    cm = a
    step = 1
    while step < L:
        cm = jnp.maximum(cm, jnp.where(t_idx >= step, pltpu.roll(cm, step, 0), -jnp.inf))
        step *= 2
    m_st = m_ref[...]
    mx = jnp.maximum(cm, m_st)
    mx_last = mx[L - 1:L, :]
    m_ref[...] = b[L - 1:L, :] + mx_last
    scale = A_DQK ** -0.5
    decay = jnp.exp(m_st - mx_last)
    w_inter = jnp.exp(m_st - mx) * scale
    e_negm = jnp.exp(-(b + mx))
    a_t = a.T
    wk_t = jnp.exp(a - mx_last).T
    k_t = k_ref[...].astype(F32).T
    ones_col = (lax.broadcasted_iota(jnp.int32, (L, A_DV), 1) == 0).astype(MXU_DTYPE)

    heads = range(A_HEADS)
    qs = [q_ref[:, hd * A_DQK:(hd + 1) * A_DQK].astype(MXU_DTYPE) for hd in heads]
    kts = [k_t[hd * A_DQK:(hd + 1) * A_DQK, :] for hd in heads]
    vs = [v_ref[:, hd * A_DV:(hd + 1) * A_DV].astype(MXU_DTYPE) for hd in heads]
    cs = [c_ref[hd] for hd in heads]
    dmats = [jnp.exp(jnp.where(causal, a_t[hd:hd + 1, :] - mx[:, hd:hd + 1], -jnp.inf)) for hd in heads]
    ss = [_mm(qs[hd], kts[hd]) * scale * dmats[hd] for hd in heads]
    inters = [_mm(qs[hd], cs[hd]) for hd in heads]
    for hd in heads:
        v_ext = jnp.concatenate([vs[hd], ones_col], axis=1)
        c_ref[hd] = decay[:, hd:hd + 1] * cs[hd] + _mm(kts[hd] * wk_t[hd:hd + 1, :], v_ext)
    wis = [w_inter[:, hd:hd + 1] for hd in heads]
    nums = [_mm(ss[hd], vs[hd]) + wis[hd] * inters[hd][:, :A_DV] for hd in heads]
    dens = [jnp.sum(ss[hd], axis=1, keepdims=True) + wis[hd] * inters[hd][:, A_DV:A_DV + 1] for hd in heads]
    outs = [nums[hd] * (1.0 / jnp.maximum(jnp.abs(dens[hd]), e_negm[:, hd:hd + 1])) for hd in heads]
    for hd in heads:
        hcell = _rms(outs[hd], hg_ref[:, hd * A_DV:(hd + 1) * A_DV])
        gate = o_ref[:, hd * A_DV:(hd + 1) * A_DV]
        y_ref[:, hd * A_DV:(hd + 1) * A_DV] = (gate * hcell).astype(y_ref.dtype)


def _mlstm_core(qk, v, o, gates, bias, head_gain, bsz, seq, rb=1):
    d = A_HEADS * A_DV
    qk_w = A_HEADS * A_DQK
    view = lambda x: x.reshape(bsz, seq, x.shape[-1])
    spec = lambda width, j: pl.BlockSpec((rb, CHUNK, width), lambda b, c: (b, c, j))
    y = pl.pallas_call(
        _mlstm_kernel,
        grid=(bsz // rb, seq // CHUNK),
        in_specs=[spec(qk_w, 0), spec(qk_w, 1), spec(d, 0), spec(d, 0), spec(LANES, 0),
                  _const_spec((1, LANES)),
                  _const_spec((1, d))],
        out_specs=spec(d, 0),
        out_shape=jax.ShapeDtypeStruct((bsz, seq, d), MXU_DTYPE),
        scratch_shapes=[pltpu.VMEM((rb, A_HEADS, A_DQK, 2 * A_DV), F32),
                        pltpu.VMEM((rb, 1, LANES), F32)],
        compiler_params=_params("parallel", "arbitrary"),
        name="mlstm_core",
    )(view(qk), view(qk), view(v), view(o), view(gates), bias, head_gain.reshape(1, d))
    return y.reshape(bsz * seq, d)


def _block_ref_rows(g, w):
    L, D = g.shape
    if 2 * w >= SUBLANES:
        g3 = g.reshape(L // (2 * w), 2 * w, D)
        return jnp.broadcast_to(g3[:, w - 1:w, :], g3.shape).reshape(L, D)
    g3 = g.reshape(L // SUBLANES, SUBLANES, D)
    sub = lax.broadcasted_iota(jnp.int32, g3.shape, 1)
    out = None
    for start in range(0, SUBLANES, 2 * w):
        r = jnp.broadcast_to(g3[:, start + w - 1:start + w, :], g3.shape)
        out = r if out is None else jnp.where(sub >= start, r, out)
    return out.reshape(L, D)


def _split3(x):
    hi = x.astype(MXU_DTYPE)
    r1 = x - hi.astype(F32)
    mid = r1.astype(MXU_DTYPE)
    lo = (r1 - mid.astype(F32)).astype(MXU_DTYPE)
    return jnp.concatenate([hi, mid, lo], axis=1)


def _dot3(mat, x3, n):
    r = jnp.dot(mat, x3, preferred_element_type=F32)
    return r[:, 0:n] + r[:, n:2 * n] + r[:, 2 * n:3 * n]


def _hgrn2_kernel(q_ref, lf_ref, k_ref, i_ref, sg_ref, hg_ref, y_ref, st_ref):
    L, D = q_ref.shape
    dk = B_DK

    @pl.when(pl.program_id(1) == 0)
    def _():
        st_ref[...] = jnp.zeros(st_ref.shape, F32)

    row = lax.broadcasted_iota(jnp.int32, (L, L), 0)
    col = lax.broadcasted_iota(jnp.int32, (L, L), 1)
    trow = lax.broadcasted_iota(jnp.int32, (L, 1), 0)
    diag = row == col

    q = q_ref[...]
    k = k_ref[...]
    g = _dot3((row >= col).astype(MXU_DTYPE), _split3(lf_ref[...]), D)

    attn = []
    for hd in range(B_HEADS):
        sl = slice(hd * dk, (hd + 1) * dk)
        attn.append(jnp.where(diag, jnp.sum(q[:, sl] * k[:, sl], axis=1, keepdims=True), 0.0))
    w = L // 2
    while w >= 1:
        upper = (trow % (2 * w)) >= w
        ref = _block_ref_rows(g, w)
        x = jnp.exp(jnp.where(upper, g - ref, ref - g))
        z = (jnp.where(upper, q, k) * x).astype(MXU_DTYPE)
        mask = ((row // (2 * w)) == (col // (2 * w))) & ((row % (2 * w)) >= w) & ((col % (2 * w)) < w)
        for hd in range(B_HEADS):
            zh = z[:, hd * dk:(hd + 1) * dk]
            attn[hd] = jnp.where(mask, _mm_nt(zh, zh), attn[hd])
        w //= 2

    g_last = g[L - 1:L, :]
    q_dec = (q * jnp.exp(g)).astype(MXU_DTYPE)
    k_dec = (k * jnp.exp(g_last - g)).astype(MXU_DTYPE)
    s_dec = jnp.exp(g_last)
    heads = range(B_HEADS)
    sls = [slice(hd * dk, (hd + 1) * dk) for hd in heads]
    vs = [i_ref[:, sl] for sl in sls]
    sts = [st_ref[hd] for hd in heads]
    outs = [_mm(attn[hd], vs[hd]) + _mm_nt(q_dec[:, sls[hd]], sts[hd]) for hd in heads]
    for hd in heads:
        st_ref[hd] = s_dec[:, sls[hd]] * sts[hd] + _mm(vs[hd].astype(F32).T, k_dec[:, sls[hd]])
    for hd in heads:
        y = _rms(outs[hd], hg_ref[:, sls[hd]]) * sg_ref[:, sls[hd]]
        y_ref[:, sls[hd]] = y.astype(y_ref.dtype)


def _hgrn2_core(q, log_f, key, iv, silu_g, head_gain, bsz, seq):
    m, d = q.shape
    nc = seq // CHUNK
    spec = pl.BlockSpec((CHUNK, d), lambda b, c: (b * nc + c, 0))
    return pl.pallas_call(
        _hgrn2_kernel,
        grid=(bsz, nc),
        in_specs=[spec, spec, spec, spec, spec, _const_spec((1, d))],
        out_specs=spec,
        out_shape=jax.ShapeDtypeStruct((m, d), MXU_DTYPE),
        scratch_shapes=[pltpu.VMEM((B_HEADS, B_DK, B_DK), F32)],
        compiler_params=_params("parallel", "arbitrary"),
        name="hgrn2_core",
    )(q, log_f, key, iv, silu_g, head_gain.reshape(1, d))


def _gmlp_kernel(u_ref, v_ref, lng_ref, lnb_ref, ws_ref, bs_ref, y_ref):
    L = u_ref.shape[0]
    half = u_ref.shape[1]
    gw = half // C_GROUPS
    v = v_ref[...]
    mu = jnp.mean(v, axis=-1, keepdims=True)
    vc = v - mu
    var = jnp.mean(vc * vc, axis=-1, keepdims=True)
    vn = (vc * lax.rsqrt(var + NORM_EPS) * lng_ref[...] + lnb_ref[...]).astype(MXU_DTYPE)
    causal = _tril_mask(L)
    for g in range(C_GROUPS):
        sl = slice(g * gw, (g + 1) * gw)
        w = jnp.where(causal, ws_ref[g], 0.0)
        vm = _mm(w, vn[:, sl]) + bs_ref[:, g:g + 1]
        y_ref[:, sl] = (u_ref[:, sl] * vm).astype(y_ref.dtype)


def _gmlp_core(z, ln_gain, ln_bias, w_s, b_s):
    m = z.shape[0]
    half = z.shape[1] // 2
    return pl.pallas_call(
        _gmlp_kernel,
        grid=(m // CHUNK,),
        in_specs=[pl.BlockSpec((CHUNK, half), lambda i: (i, 0)),
                  pl.BlockSpec((CHUNK, half), lambda i: (i, 1)),
                  _const_spec((1, half)),
                  _const_spec((1, half)),
                  _const_spec(w_s.shape),
                  _const_spec((CHUNK, C_GROUPS))],
        out_specs=pl.BlockSpec((CHUNK, half), lambda i: (i, 0)),
        out_shape=jax.ShapeDtypeStruct((m, half), MXU_DTYPE),
        compiler_params=_params("parallel"),
        name="gmlp_core",
    )(z, z, ln_gain.reshape(1, half), ln_bias.reshape(1, half), w_s, b_s.T)


def _rglru_kernel(gb_ref, xb_ref, cw_ref, vec_ref, wa_ref, wx_ref, y_ref, xpad_ref, hc_ref):
    T, d = xb_ref.shape
    halo = xpad_ref.shape[0] - T

    @pl.when(pl.program_id(1) == 0)
    def _():
        xpad_ref[0:halo, :] = jnp.zeros((halo, d), F32)
        hc_ref[...] = jnp.zeros(hc_ref.shape, F32)

    xb = xb_ref[...]
    xpad_ref[halo:, :] = xb
    xc = xb * cw_ref[D_CONV - 1:D_CONV, :] + vec_ref[0:1, :]
    for j in range(1, D_CONV):
        shifted = xpad_ref[halo - j:halo - j + T, :]
        xc = xc + shifted * cw_ref[D_CONV - 1 - j:D_CONV - j, :]
    xpad_ref[0:halo, :] = xb[T - halo:T, :]

    bw = d // D_BLOCKS
    r_parts, i_parts = [], []
    for n in range(D_BLOCKS):
        xn = xc[:, n * bw:(n + 1) * bw]
        r_parts.append(_mm(xn, wa_ref[n]))
        i_parts.append(_mm(xn, wx_ref[n]))
    r = _sigmoid(jnp.concatenate(r_parts, axis=1) + vec_ref[1:2, :])
    ig = _sigmoid(jnp.concatenate(i_parts, axis=1) + vec_ref[2:3, :])
    nlam = -vec_ref[3:4, :]
    softplus = jnp.maximum(nlam, 0.0) + jnp.log1p(jnp.exp(-jnp.abs(nlam)))
    a = jnp.exp(-RG_C * r * softplus)
    om = 1.0 - a * a
    b = (om * lax.rsqrt(jnp.maximum(om, SQRT_FLOOR))) * (ig * xc)

    groups = T // SUBLANES
    a = a.reshape(groups, SUBLANES, d)
    b = b.reshape(groups, SUBLANES, d)
    sub = lax.broadcasted_iota(jnp.int32, a.shape, 1)
    step = 1
    while step < SUBLANES:
        keep = sub >= step
        a_prev = jnp.where(keep, pltpu.roll(a, step, 1), 1.0)
        b_prev = jnp.where(keep, pltpu.roll(b, step, 1), 0.0)
        b = a * b_prev + b
        a = a * a_prev
        step *= 2
    carry = hc_ref[...]
    rows = []
    for g in range(groups):
        hg = b[g] + a[g] * carry
        carry = hg[SUBLANES - 1:SUBLANES, :]
        rows.append(hg)
    hc_ref[...] = carry
    y_ref[...] = (jnp.concatenate(rows, axis=0) * gb_ref[...]).astype(y_ref.dtype)


def _rglru_core(gate, xb, conv_w, conv_b, w_a, b_a, w_x, b_x, lam, bsz, seq, tt=256):
    m, d = xb.shape
    nt = seq // tt
    row = lambda b, t: b * nt + t
    vecs = jnp.stack([conv_b, b_a, b_x, lam])
    return pl.pallas_call(
        _rglru_kernel,
        grid=(bsz, nt),
        in_specs=[pl.BlockSpec((tt, d), lambda b, t: (row(b, t), 0)),
                  pl.BlockSpec((tt, d), lambda b, t: (row(b, t), 0)),
                  _const_spec(conv_w.shape),
                  _const_spec(vecs.shape),
                  _const_spec(w_a.shape),
                  _const_spec(w_x.shape)],
        out_specs=pl.BlockSpec((tt, d), lambda b, t: (row(b, t), 0)),
        out_shape=jax.ShapeDtypeStruct((m, d), MXU_DTYPE),
        scratch_shapes=[pltpu.VMEM((SUBLANES + tt, d), F32), pltpu.VMEM((1, d), F32)],
        compiler_params=_params("parallel", "arbitrary"),
        name="rglru_core",
    )(gate, xb, conv_w, vecs, w_a.astype(MXU_DTYPE), w_x.astype(MXU_DTYPE))


def kernel(x, p, norm_gains, mlp_w_up, mlp_w_down, ple_w_up, ple_w_gate, a_w_in, a_ig_bias, a_fg_bias, a_head_gain, a_w_out, b_w_in, b_lower_bound, b_head_gain, b_w_out, c_w_in, c_ln_gain, c_ln_bias, c_spatial_w, c_spatial_b, c_w_out, d_w_in, d_conv_w, d_conv_b, d_w_a, d_b_a, d_w_x, d_b_x, d_lambda, d_w_out):
    bsz, seq, d = x.shape
    depth = norm_gains.shape[0]
    n_mixers = 4
    h = x.reshape(bsz * seq, d)
    for i in range(depth):
        kind, j = i % n_mixers, i // n_mixers
        g = norm_gains[i]
        if kind == 0:
            w_in = jnp.pad(a_w_in[j], ((0, 0), (0, LANES - 2 * A_HEADS)))
            qk, v, o_gate, gates = _norm_matmul(
                h, g[0], w_in,
                [(2 * A_HEADS * A_DQK, None, (MXU_DTYPE,)), (d, None, (MXU_DTYPE,)),
                 (d, _ep_sigmoid, (F32,)), (LANES, None, (F32,))])
            bias = jnp.pad(jnp.concatenate([a_ig_bias[j], a_fg_bias[j]]), (0, LANES - 2 * A_HEADS))
            y = _mlstm_core(qk, v, o_gate, gates, bias.reshape(1, LANES), a_head_gain[j], bsz, seq)
            w_out = a_w_out[j]
        elif kind == 1:
            gate_ep = functools.partial(_hgrn2_gate_epilogue, layer=i)
            q, log_f, key, iv, silu_g = _norm_matmul(
                h, g[0], b_w_in[j],
                [(d, None, (F32,)), (d, gate_ep, (F32, F32)), (d, None, (MXU_DTYPE,)), (d, _ep_silu, (F32,))],
                aux=b_lower_bound)
            y = _hgrn2_core(q, log_f, key, iv, silu_g, b_head_gain[j], bsz, seq)
            w_out = b_w_out[j]
        elif kind == 2:
            z, = _norm_matmul(h, g[0], c_w_in[j], [(c_w_in.shape[2], _ep_gelu, (F32,))])
            y = _gmlp_core(z, c_ln_gain[j], c_ln_bias[j], c_spatial_w[j], c_spatial_b[j])
            w_out = c_w_out[j]
        else:
            gate, xb = _norm_matmul(h, g[0], d_w_in[j], [(d, _ep_gelu, (F32,)), (d, None, (F32,))])
            y = _rglru_core(gate, xb, d_conv_w[j], d_conv_b[j], d_w_a[j], d_b_a[j], d_w_x[j], d_b_x[j],
                            d_lambda[j], bsz, seq)
            w_out = d_w_out[j]
        h = _post_mixer(h, y, p[i].reshape(bsz * seq, -1), g[1:5], w_out, mlp_w_up[i], mlp_w_down[i],
                        ple_w_gate[i], ple_w_up[i])
    return h.reshape(bsz, seq, d)
```

```python
import functools

import jax
import jax.numpy as jnp
from jax import lax
from jax.experimental import pallas as pl
from jax.experimental.pallas import tpu as pltpu

F32 = jnp.float32
MXU_DTYPE = jnp.bfloat16
NORM_EPS = 1e-6
RG_C = 8.0

LANES = 128
SUBLANES = 8
SQRT_FLOOR = 1e-30
CHUNK = 128
VMEM_LIMIT = 56 * 1024 * 1024

A_HEADS, A_DQK, A_DV = 8, 64, 128
B_HEADS, B_DK = 8, 128
C_GROUPS = 8
D_BLOCKS, D_CONV = 4, 4


def _params(*sem):
    return pltpu.CompilerParams(dimension_semantics=sem, vmem_limit_bytes=VMEM_LIMIT)


def _mm(a, b):
    return jnp.dot(a.astype(MXU_DTYPE), b.astype(MXU_DTYPE), preferred_element_type=F32)


def _mm_nt(a, b):
    return lax.dot_general(a.astype(MXU_DTYPE), b.astype(MXU_DTYPE),
                           (((1,), (1,)), ((), ())), preferred_element_type=F32)


def _rms(x, gain):
    return x * lax.rsqrt(jnp.mean(x * x, axis=-1, keepdims=True) + NORM_EPS) * gain


def _sigmoid(x):
    return 0.5 * jnp.tanh(0.5 * x) + 0.5


def _log_sigmoid(x):
    return jnp.minimum(x, 0.0) - jnp.log(1.0 + jnp.exp(-jnp.abs(x)))


_GELU_K = 0.7978845608028654


def _gelu_tanh(x):
    hx = 0.5 * x
    return hx + hx * jnp.tanh(x * (_GELU_K + (_GELU_K * 0.044715) * (x * x)))


def _hgrn2_gates(fz, lbp, layer):
    e = jnp.exp(lbp - jnp.max(lbp, axis=0, keepdims=True))
    sm = e / jnp.sum(e, axis=0, keepdims=True)
    cs = sm[0:1, :]
    first = cs
    for r in range(1, layer + 1):
        cs = cs + sm[r:r + 1, :]
    lb = cs - first
    t = jnp.exp(-jnp.abs(fz))
    r = 1.0 / (1.0 + t)
    sig_abs, sig_nabs = r, t * r
    pos = fz >= 0.0
    sig = jnp.where(pos, sig_abs, sig_nabs)
    sig_neg = jnp.where(pos, sig_nabs, sig_abs)
    return jnp.log(lb + (1.0 - lb) * sig), (1.0 - lb) * sig_neg


def _const_spec(shape):
    nd = len(shape)
    return pl.BlockSpec(shape, lambda *_: (0,) * nd)


def _single_buffered(shape):
    nd = len(shape)
    return pl.BlockSpec(shape, lambda *_: (0,) * nd, pipeline_mode=pl.Buffered(1))


def _post_mixer_kernel(h_ref, y_ref, p_ref, g_ref, wo_ref, wu_ref, wd_ref, wg_ref, wp_ref, o_ref,
                       *, ff_chunk, sub_rows):
    n_ff = wu_ref.shape[1] // ff_chunk
    n_sub = h_ref.shape[0] // sub_rows
    rows = [slice(r * sub_rows, (r + 1) * sub_rows) for r in range(n_sub)]

    def head(r):
        h = h_ref[rows[r], :] + _rms(jnp.dot(y_ref[rows[r], :], wo_ref[...], preferred_element_type=F32),
                                     g_ref[0:1, :])
        return h, _rms(h, g_ref[1:2, :]).astype(MXU_DTYPE), jnp.zeros(h.shape, F32)

    def ff(hn, acc, c):
        u = jnp.dot(hn, wu_ref[:, c * ff_chunk:(c + 1) * ff_chunk], preferred_element_type=F32)
        u = jnp.maximum(u, 0.0)
        u = (u * u).astype(MXU_DTYPE)
        return acc + jnp.dot(u, wd_ref[c * ff_chunk:(c + 1) * ff_chunk, :], preferred_element_type=F32)

    def tail(r, h, acc):
        h = h + _rms(acc, g_ref[2:3, :])
        gate = _sigmoid(jnp.dot(h.astype(MXU_DTYPE), wg_ref[...], preferred_element_type=F32))
        emb = jnp.dot(p_ref[rows[r], :].astype(MXU_DTYPE), wp_ref[...], preferred_element_type=F32)
        o_ref[rows[r], :] = h + _rms(gate * emb, g_ref[3:4, :])

    half = n_ff // 2
    prev = None
    for r in range(n_sub):
        h, hn, acc = head(r)
        if prev is not None:
            ph, phn, pacc = prev
            for c in range(half, n_ff):
                pacc = ff(phn, pacc, c)
        for c in range(half):
            acc = ff(hn, acc, c)
        if prev is not None:
            tail(r - 1, ph, pacc)
        prev = (h, hn, acc)
    ph, phn, pacc = prev
    for c in range(half, n_ff):
        pacc = ff(phn, pacc, c)
    tail(n_sub - 1, ph, pacc)


def _post_mixer(h, y, p, gains4, w_out, w_up, w_down, w_gate, w_ple, tm=512, sub_rows=256, ff_chunk=1024):
    m, d = h.shape
    k = y.shape[1]
    dp = p.shape[1]
    weights = [w.astype(MXU_DTYPE) for w in (w_out, w_up, w_down, w_gate, w_ple)]
    return pl.pallas_call(
        functools.partial(_post_mixer_kernel, ff_chunk=ff_chunk, sub_rows=sub_rows),
        grid=(m // tm,),
        in_specs=[pl.BlockSpec((tm, d), lambda i: (i, 0)),
                  pl.BlockSpec((tm, k), lambda i: (i, 0)),
                  pl.BlockSpec((tm, dp), lambda i: (i, 0)),
                  _const_spec((4, d))] + [_single_buffered(w.shape) for w in weights],
        out_specs=pl.BlockSpec((tm, d), lambda i: (i, 0)),
        out_shape=jax.ShapeDtypeStruct((m, d), F32),
        compiler_params=_params("parallel"),
        name="post_mixer",
    )(h, y, p, gains4, *weights)


def _tril_mask(n):
    r = lax.broadcasted_iota(jnp.int32, (n, n), 0)
    c = lax.broadcasted_iota(jnp.int32, (n, n), 1)
    return r >= c


def _mlstm_gates(pre, m_ref):
    L = pre.shape[0]
    lane = lax.broadcasted_iota(jnp.int32, (L, LANES), 1)
    t_idx = lax.broadcasted_iota(jnp.int32, (L, LANES), 0)
    val = jnp.where(lane < A_HEADS, pre, jnp.where(lane < 2 * A_HEADS, _log_sigmoid(pre), 0.0))
    bc = jnp.dot(_tril_mask(L).astype(F32), val, preferred_element_type=F32,
                 precision=lax.Precision.HIGHEST)
    b = pltpu.roll(bc, LANES - A_HEADS, 1)
    a = val - b
    cm = a
    step = 1
    while step < L:
        cm = jnp.maximum(cm, jnp.where(t_idx >= step, pltpu.roll(cm, step, 0), -jnp.inf))
        step *= 2
    m_st = m_ref[...]
    mx = jnp.maximum(cm, m_st)
    mx_last = mx[L - 1:L, :]
    m_ref[...] = b[L - 1:L, :] + mx_last
    return dict(
        mx=mx,
        a_t=a.T,
        wk_t=jnp.exp(a - mx_last).T,
        decay=jnp.exp(m_st - mx_last),
        w_inter=jnp.exp(m_st - mx) * A_DQK ** -0.5,
        e_negm=jnp.exp(-(b + mx)))


def _mlstm_heads(gt, q, k, v, o_gate, hg_ref, c_ref, y_ref, rows):
    L = q.shape[0]
    causal = _tril_mask(L)
    scale = A_DQK ** -0.5
    k_t = k.T
    ones_col = (lax.broadcasted_iota(jnp.int32, (L, A_DV), 1) == 0).astype(MXU_DTYPE)
    heads = range(A_HEADS)
    qs = [q[:, hd * A_DQK:(hd + 1) * A_DQK].astype(MXU_DTYPE) for hd in heads]
    kts = [k_t[hd * A_DQK:(hd + 1) * A_DQK, :] for hd in heads]
    vs = [v[:, hd * A_DV:(hd + 1) * A_DV].astype(MXU_DTYPE) for hd in heads]
    cs = [c_ref[hd] for hd in heads]
    dmats = [jnp.exp(jnp.where(causal, gt["a_t"][hd:hd + 1, :] - gt["mx"][:, hd:hd + 1], -jnp.inf))
             for hd in heads]
    ss = [_mm(qs[hd], kts[hd]) * scale * dmats[hd] for hd in heads]
    inters = [_mm(qs[hd], cs[hd]) for hd in heads]
    for hd in heads:
        v_ext = jnp.concatenate([vs[hd], ones_col], axis=1)
        c_ref[hd] = (gt["decay"][:, hd:hd + 1] * cs[hd]
                     + _mm(kts[hd] * gt["wk_t"][hd:hd + 1, :], v_ext))
    wis = [gt["w_inter"][:, hd:hd + 1] for hd in heads]
    nums = [_mm(ss[hd], vs[hd]) + wis[hd] * inters[hd][:, :A_DV] for hd in heads]
    dens = [jnp.sum(ss[hd], axis=1, keepdims=True) + wis[hd] * inters[hd][:, A_DV:A_DV + 1] for hd in heads]
    outs = [nums[hd] * (1.0 / jnp.maximum(jnp.abs(dens[hd]), gt["e_negm"][:, hd:hd + 1])) for hd in heads]
    for hd in heads:
        sl = slice(hd * A_DV, (hd + 1) * A_DV)
        y_ref[rows, sl] = (o_gate[:, sl] * _rms(outs[hd], hg_ref[:, sl])).astype(y_ref.dtype)


def _mlstm_kernel(h_ref, g_ref, w_ref, bias_ref, hg_ref, y_ref, c_ref, m_ref):
    tm, d = y_ref.shape
    qk_w = A_HEADS * A_DQK

    @pl.when(pl.program_id(1) == 0)
    def _():
        c_ref[...] = jnp.zeros(c_ref.shape, F32)
        m_ref[...] = jnp.zeros(m_ref.shape, F32)

    hn = _rms(h_ref[...], g_ref[...]).astype(MXU_DTYPE)
    proj = lambda lo, hi: jnp.dot(hn, w_ref[:, lo:hi], preferred_element_type=F32)
    pre = proj(2 * qk_w + 2 * d, 2 * qk_w + 2 * d + LANES) + bias_ref[...]
    q = proj(0, qk_w)
    k = proj(qk_w, 2 * qk_w)
    chunks = [slice(c * CHUNK, (c + 1) * CHUNK) for c in range(tm // CHUNK)]
    gt = _mlstm_gates(pre[chunks[0], :], m_ref)
    v = proj(2 * qk_w, 2 * qk_w + d)
    o_gate = _sigmoid(proj(2 * qk_w + d, 2 * qk_w + 2 * d))
    for c, rows in enumerate(chunks):
        if c > 0:
            gt = _mlstm_gates(pre[rows, :], m_ref)
        _mlstm_heads(gt, q[rows, :], k[rows, :], v[rows, :], o_gate[rows, :], hg_ref, c_ref, y_ref, rows)


def _mlstm_mixer(h, gain, w_in, ig_bias, fg_bias, head_gain, bsz, seq, tm=2 * CHUNK):
    m, dm = h.shape
    d = A_HEADS * A_DV
    nt = seq // tm
    pad = LANES - 2 * A_HEADS
    w_in = jnp.pad(w_in, ((0, 0), (0, pad)))
    bias = jnp.pad(jnp.concatenate([ig_bias, fg_bias]), (0, pad)).reshape(1, LANES)
    return pl.pallas_call(
        _mlstm_kernel,
        grid=(bsz, nt),
        in_specs=[pl.BlockSpec((tm, dm), lambda b, t: (b * nt + t, 0)),
                  _const_spec((1, dm)),
                  _single_buffered(w_in.shape),
                  _const_spec((1, LANES)),
                  _const_spec((1, d))],
        out_specs=pl.BlockSpec((tm, d), lambda b, t: (b * nt + t, 0)),
        out_shape=jax.ShapeDtypeStruct((m, d), MXU_DTYPE),
        scratch_shapes=[pltpu.VMEM((A_HEADS, A_DQK, 2 * A_DV), F32),
                        pltpu.VMEM((1, LANES), F32)],
        compiler_params=_params("parallel", "arbitrary"),
        name="mlstm_mixer",
    )(h, gain.reshape(1, dm), w_in.astype(MXU_DTYPE), bias, head_gain.reshape(1, d))


def _block_ref_rows(g, w):
    L, D = g.shape
    if 2 * w >= SUBLANES:
        g3 = g.reshape(L // (2 * w), 2 * w, D)
        return jnp.broadcast_to(g3[:, w - 1:w, :], g3.shape).reshape(L, D)
    g3 = g.reshape(L // SUBLANES, SUBLANES, D)
    sub = lax.broadcasted_iota(jnp.int32, g3.shape, 1)
    out = None
    for start in range(0, SUBLANES, 2 * w):
        r = jnp.broadcast_to(g3[:, start + w - 1:start + w, :], g3.shape)
        out = r if out is None else jnp.where(sub >= start, r, out)
    return out.reshape(L, D)


def _split3(x):
    hi = x.astype(MXU_DTYPE)
    r1 = x - hi.astype(F32)
    mid = r1.astype(MXU_DTYPE)
    lo = (r1 - mid.astype(F32)).astype(MXU_DTYPE)
    return jnp.concatenate([hi, mid, lo], axis=1)


def _dot3(mat, x3, n):
    r = jnp.dot(mat, x3, preferred_element_type=F32)
    return r[:, 0:n] + r[:, n:2 * n] + r[:, 2 * n:3 * n]


def _hgrn2_attn(q, log_f, k):
    L, D = q.shape
    dk = B_DK
    row = lax.broadcasted_iota(jnp.int32, (L, L), 0)
    col = lax.broadcasted_iota(jnp.int32, (L, L), 1)
    trow = lax.broadcasted_iota(jnp.int32, (L, 1), 0)
    diag = row == col
    g = _dot3((row >= col).astype(MXU_DTYPE), _split3(log_f), D)

    attn = []
    for hd in range(B_HEADS):
        sl = slice(hd * dk, (hd + 1) * dk)
        attn.append(jnp.where(diag, jnp.sum(q[:, sl] * k[:, sl], axis=1, keepdims=True), 0.0))
    w = L // 2
    while w >= 1:
        upper = (trow % (2 * w)) >= w
        ref = _block_ref_rows(g, w)
        x = jnp.exp(jnp.where(upper, g - ref, ref - g))
        z = (jnp.where(upper, q, k) * x).astype(MXU_DTYPE)
        mask = ((row // (2 * w)) == (col // (2 * w))) & ((row % (2 * w)) >= w) & ((col % (2 * w)) < w)
        for hd in range(B_HEADS):
            zh = z[:, hd * dk:(hd + 1) * dk]
            attn[hd] = jnp.where(mask, _mm_nt(zh, zh), attn[hd])
        w //= 2
    return attn, g


def _hgrn2_tail(attn, g, q, k, iv, silu_g, hg_ref, st_ref, y_ref, rows):
    L = q.shape[0]
    dk = B_DK
    g_last = g[L - 1:L, :]
    q_dec = (q * jnp.exp(g)).astype(MXU_DTYPE)
    k_dec = (k * jnp.exp(g_last - g)).astype(MXU_DTYPE)
    s_dec = jnp.exp(g_last)
    heads = range(B_HEADS)
    sls = [slice(hd * dk, (hd + 1) * dk) for hd in heads]
    vs = [iv[:, sl] for sl in sls]
    sts = [st_ref[hd] for hd in heads]
    outs = [_mm(attn[hd], vs[hd]) + _mm_nt(q_dec[:, sls[hd]], sts[hd]) for hd in heads]
    for hd in heads:
        st_ref[hd] = s_dec[:, sls[hd]] * sts[hd] + _mm(vs[hd].T, k_dec[:, sls[hd]])
    for hd in heads:
        y = _rms(outs[hd], hg_ref[:, sls[hd]]) * silu_g[:, sls[hd]]
        y_ref[rows, sls[hd]] = y.astype(y_ref.dtype)


def _hgrn2_kernel(h_ref, g_ref, w_ref, lbp_ref, hg_ref, y_ref, st_ref, *, layer):
    tm, d = y_ref.shape

    @pl.when(pl.program_id(1) == 0)
    def _():
        st_ref[...] = jnp.zeros(st_ref.shape, F32)

    hn = _rms(h_ref[...], g_ref[...]).astype(MXU_DTYPE)
    proj = lambda j: jnp.dot(hn, w_ref[:, j * d:(j + 1) * d], preferred_element_type=F32)
    q = proj(0)
    log_f, k = _hgrn2_gates(proj(1), lbp_ref[...], layer)
    chunks = [slice(c * CHUNK, (c + 1) * CHUNK) for c in range(tm // CHUNK)]
    attn, g = _hgrn2_attn(q[chunks[0], :], log_f[chunks[0], :], k[chunks[0], :])
    iv = proj(2)
    gz = proj(3)
    silu_g = gz * _sigmoid(gz)
    for c, rows in enumerate(chunks):
        if c > 0:
            attn, g = _hgrn2_attn(q[rows, :], log_f[rows, :], k[rows, :])
        _hgrn2_tail(attn, g, q[rows, :], k[rows, :], iv[rows, :], silu_g[rows, :], hg_ref, st_ref, y_ref, rows)


def _hgrn2_mixer(h, gain, w_in, lower_bound, head_gain, layer, bsz, seq, tm=2 * CHUNK):
    m, dm = h.shape
    d = B_HEADS * B_DK
    nt = seq // tm
    return pl.pallas_call(
        functools.partial(_hgrn2_kernel, layer=layer),
        grid=(bsz, nt),
        in_specs=[pl.BlockSpec((tm, dm), lambda b, t: (b * nt + t, 0)),
                  _const_spec((1, dm)),
                  _single_buffered(w_in.shape),
                  _const_spec(lower_bound.shape),
                  _const_spec((1, d))],
        out_specs=pl.BlockSpec((tm, d), lambda b, t: (b * nt + t, 0)),
        out_shape=jax.ShapeDtypeStruct((m, d), MXU_DTYPE),
        scratch_shapes=[pltpu.VMEM((B_HEADS, B_DK, B_DK), F32)],
        compiler_params=_params("parallel", "arbitrary"),
        name="hgrn2_mixer",
    )(h, gain.reshape(1, dm), w_in.astype(MXU_DTYPE), lower_bound, head_gain.reshape(1, d))


def _gmlp_kernel(h_ref, g_ref, w_ref, lng_ref, lnb_ref, ws_ref, bs_ref, y_ref):
    tm = h_ref.shape[0]
    half = y_ref.shape[1]
    gw = half // C_GROUPS
    hn = _rms(h_ref[...], g_ref[...]).astype(MXU_DTYPE)
    v = _gelu_tanh(jnp.dot(hn, w_ref[:, half:], preferred_element_type=F32))
    mu = jnp.mean(v, axis=-1, keepdims=True)
    vc = v - mu
    var = jnp.mean(vc * vc, axis=-1, keepdims=True)
    vn = (vc * lax.rsqrt(var + NORM_EPS) * lng_ref[...] + lnb_ref[...]).astype(MXU_DTYPE)
    causal = _tril_mask(CHUNK)
    for g in range(C_GROUPS):
        sl = slice(g * gw, (g + 1) * gw)
        u = _gelu_tanh(jnp.dot(hn, w_ref[:, sl], preferred_element_type=F32))
        w = jnp.where(causal, ws_ref[g], 0.0).astype(MXU_DTYPE)
        for c in range(tm // CHUNK):
            rows = slice(c * CHUNK, (c + 1) * CHUNK)
            vm = jnp.dot(w, vn[rows, sl], preferred_element_type=F32) + bs_ref[:, g:g + 1]
            y_ref[rows, sl] = (u[rows, :] * vm).astype(y_ref.dtype)


def _gmlp_mixer(h, gain, w_in, ln_gain, ln_bias, w_s, b_s, tm=512):
    m, d = h.shape
    half = w_in.shape[1] // 2
    return pl.pallas_call(
        _gmlp_kernel,
        grid=(m // tm,),
        in_specs=[pl.BlockSpec((tm, d), lambda i: (i, 0)),
                  _const_spec((1, d)),
                  _single_buffered(w_in.shape),
                  _const_spec((1, half)),
                  _const_spec((1, half)),
                  _const_spec(w_s.shape),
                  _const_spec((CHUNK, C_GROUPS))],
        out_specs=pl.BlockSpec((tm, half), lambda i: (i, 0)),
        out_shape=jax.ShapeDtypeStruct((m, half), MXU_DTYPE),
        compiler_params=_params("parallel"),
        name="gmlp_mixer",
    )(h, gain.reshape(1, d), w_in.astype(MXU_DTYPE), ln_gain.reshape(1, half), ln_bias.reshape(1, half),
      w_s, b_s.T)


def _rglru_kernel(h_ref, g_ref, w_ref, cw_ref, vec_ref, wa_ref, wx_ref, y_ref, xpad_ref, hc_ref):
    T, d = y_ref.shape
    halo = xpad_ref.shape[0] - T

    @pl.when(pl.program_id(1) == 0)
    def _():
        xpad_ref[0:halo, :] = jnp.zeros((halo, d), F32)
        hc_ref[...] = jnp.zeros(hc_ref.shape, F32)

    hn = _rms(h_ref[...], g_ref[...]).astype(MXU_DTYPE)
    xb = jnp.dot(hn, w_ref[:, d:], preferred_element_type=F32)
    xpad_ref[halo:, :] = xb
    xc = xb * cw_ref[D_CONV - 1:D_CONV, :] + vec_ref[0:1, :]
    for j in range(1, D_CONV):
        shifted = xpad_ref[halo - j:halo - j + T, :]
        xc = xc + shifted * cw_ref[D_CONV - 1 - j:D_CONV - j, :]
    xpad_ref[0:halo, :] = xb[T - halo:T, :]

    bw = d // D_BLOCKS
    r_parts, i_parts = [], []
    for n in range(D_BLOCKS):
        xn = xc[:, n * bw:(n + 1) * bw]
        r_parts.append(_mm(xn, wa_ref[n]))
        i_parts.append(_mm(xn, wx_ref[n]))
    gate = _gelu_tanh(jnp.dot(hn, w_ref[:, :d], preferred_element_type=F32))
    r = _sigmoid(jnp.concatenate(r_parts, axis=1) + vec_ref[1:2, :])
    ig = _sigmoid(jnp.concatenate(i_parts, axis=1) + vec_ref[2:3, :])
    nlam = -vec_ref[3:4, :]
    softplus = jnp.maximum(nlam, 0.0) + jnp.log1p(jnp.exp(-jnp.abs(nlam)))
    a = jnp.exp(-RG_C * r * softplus)
    om = 1.0 - a * a
    b = (om * lax.rsqrt(jnp.maximum(om, SQRT_FLOOR))) * (ig * xc)

    groups = T // SUBLANES
    a = a.reshape(groups, SUBLANES, d)
    b = b.reshape(groups, SUBLANES, d)
    sub = lax.broadcasted_iota(jnp.int32, a.shape, 1)
    step = 1
    while step < SUBLANES:
        keep = sub >= step
        a_prev = jnp.where(keep, pltpu.roll(a, step, 1), 1.0)
        b_prev = jnp.where(keep, pltpu.roll(b, step, 1), 0.0)
        b = a * b_prev + b
        a = a * a_prev
        step *= 2
    carry = hc_ref[...]
    rows = []
    for g in range(groups):
        hg = b[g] + a[g] * carry
        carry = hg[SUBLANES - 1:SUBLANES, :]
        rows.append(hg)
    hc_ref[...] = carry
    y_ref[...] = (jnp.concatenate(rows, axis=0) * gate).astype(y_ref.dtype)


def _rglru_mixer(h, gain, w_in, conv_w, conv_b, w_a, b_a, w_x, b_x, lam, bsz, seq, tt=256):
    m, dm = h.shape
    d = w_in.shape[1] // 2
    nt = seq // tt
    row = lambda b, t: b * nt + t
    vecs = jnp.stack([conv_b, b_a, b_x, lam])
    return pl.pallas_call(
        _rglru_kernel,
        grid=(bsz, nt),
        in_specs=[pl.BlockSpec((tt, dm), lambda b, t: (row(b, t), 0)),
                  _const_spec((1, dm)),
                  _single_buffered(w_in.shape),
                  _const_spec(conv_w.shape),
                  _const_spec(vecs.shape),
                  _const_spec(w_a.shape),
                  _const_spec(w_x.shape)],
        out_specs=pl.BlockSpec((tt, d), lambda b, t: (row(b, t), 0)),
        out_shape=jax.ShapeDtypeStruct((m, d), MXU_DTYPE),
        scratch_shapes=[pltpu.VMEM((SUBLANES + tt, d), F32), pltpu.VMEM((1, d), F32)],
        compiler_params=_params("parallel", "arbitrary"),
        name="rglru_mixer",
    )(h, gain.reshape(1, dm), w_in.astype(MXU_DTYPE), conv_w, vecs, w_a.astype(MXU_DTYPE),
      w_x.astype(MXU_DTYPE))


def kernel(x, p, norm_gains, mlp_w_up, mlp_w_down, ple_w_up, ple_w_gate, a_w_in, a_ig_bias, a_fg_bias, a_head_gain, a_w_out, b_w_in, b_lower_bound, b_head_gain, b_w_out, c_w_in, c_ln_gain, c_ln_bias, c_spatial_w, c_spatial_b, c_w_out, d_w_in, d_conv_w, d_conv_b, d_w_a, d_b_a, d_w_x, d_b_x, d_lambda, d_w_out):
    bsz, seq, d = x.shape
    depth = norm_gains.shape[0]
    n_mixers = 4
    h = x.reshape(bsz * seq, d)
    for i in range(depth):
        kind, j = i % n_mixers, i // n_mixers
        g = norm_gains[i]
        if kind == 0:
            y = _mlstm_mixer(h, g[0], a_w_in[j], a_ig_bias[j], a_fg_bias[j], a_head_gain[j], bsz, seq)
            w_out = a_w_out[j]
        elif kind == 1:
            y = _hgrn2_mixer(h, g[0], b_w_in[j], b_lower_bound, b_head_gain[j], i, bsz, seq)
            w_out = b_w_out[j]
        elif kind == 2:
            y = _gmlp_mixer(h, g[0], c_w_in[j], c_ln_gain[j], c_ln_bias[j], c_spatial_w[j], c_spatial_b[j])
            w_out = c_w_out[j]
        else:
            y = _rglru_mixer(h, g[0], d_w_in[j], d_conv_w[j], d_conv_b[j], d_w_a[j], d_b_a[j], d_w_x[j],
                             d_b_x[j], d_lambda[j], bsz, seq)
            w_out = d_w_out[j]
        h = _post_mixer(h, y, p[i].reshape(bsz * seq, -1), g[1:5], w_out, mlp_w_up[i], mlp_w_down[i],
                        ple_w_gate[i], ple_w_up[i])
    return h.reshape(bsz, seq, d)
```

```python
import functools

import jax
import jax.numpy as jnp
from jax import lax
from jax.experimental import pallas as pl
from jax.experimental.pallas import tpu as pltpu

F32 = jnp.float32
MXU_DTYPE = jnp.bfloat16
NORM_EPS = 1e-6
RG_C = 8.0

LANES = 128
SUBLANES = 8
SQRT_FLOOR = 1e-30
CHUNK = 128
VMEM_LIMIT = 63 * 1024 * 1024

A_HEADS, A_DQK, A_DV = 8, 64, 128
B_HEADS, B_DK = 8, 128
C_GROUPS = 8
D_BLOCKS, D_CONV = 4, 4


def _params(*sem):
    return pltpu.CompilerParams(dimension_semantics=sem, vmem_limit_bytes=VMEM_LIMIT)


def _mm(a, b):
    return jnp.dot(a.astype(MXU_DTYPE), b.astype(MXU_DTYPE), preferred_element_type=F32)


def _mm_nt(a, b):
    return lax.dot_general(a.astype(MXU_DTYPE), b.astype(MXU_DTYPE),
                           (((1,), (1,)), ((), ())), preferred_element_type=F32)


def _rms(x, gain):
    return x * lax.rsqrt(jnp.mean(x * x, axis=-1, keepdims=True) + NORM_EPS) * gain


def _sigmoid(x):
    return 0.5 * jnp.tanh(0.5 * x) + 0.5


def _log_sigmoid(x):
    return jnp.minimum(x, 0.0) - jnp.log(1.0 + jnp.exp(-jnp.abs(x)))


_GELU_K = 0.7978845608028654


def _gelu_tanh(x):
    hx = 0.5 * x
    return hx + hx * jnp.tanh(x * (_GELU_K + (_GELU_K * 0.044715) * (x * x)))


def _hgrn2_gates(fz, lbp, layer):
    e = jnp.exp(lbp - jnp.max(lbp, axis=0, keepdims=True))
    sm = e / jnp.sum(e, axis=0, keepdims=True)
    cs = sm[0:1, :]
    first = cs
    for r in range(1, layer + 1):
        cs = cs + sm[r:r + 1, :]
    lb = cs - first
    t = jnp.exp(-jnp.abs(fz))
    r = 1.0 / (1.0 + t)
    sig_abs, sig_nabs = r, t * r
    pos = fz >= 0.0
    sig = jnp.where(pos, sig_abs, sig_nabs)
    sig_neg = jnp.where(pos, sig_nabs, sig_abs)
    return jnp.log(lb + (1.0 - lb) * sig), (1.0 - lb) * sig_neg


def _const_spec(shape):
    nd = len(shape)
    return pl.BlockSpec(shape, lambda *_: (0,) * nd)


def _single_buffered(shape):
    nd = len(shape)
    return pl.BlockSpec(shape, lambda *_: (0,) * nd, pipeline_mode=pl.Buffered(1))


def _post_mixer_kernel(h_ref, y_ref, p_ref, g_ref, wo_ref, wu_ref, wd_ref, wg_ref, wp_ref, o_ref,
                       *, ff_chunk, sub_rows):
    n_ff = wu_ref.shape[1] // ff_chunk
    n_sub = h_ref.shape[0] // sub_rows
    rows = [slice(r * sub_rows, (r + 1) * sub_rows) for r in range(n_sub)]

    def head(r):
        h = h_ref[rows[r], :] + _rms(jnp.dot(y_ref[rows[r], :], wo_ref[...], preferred_element_type=F32),
                                     g_ref[0:1, :])
        return h, _rms(h, g_ref[1:2, :]).astype(MXU_DTYPE), jnp.zeros(h.shape, F32)

    def ff(hn, acc, c):
        u = jnp.dot(hn, wu_ref[:, c * ff_chunk:(c + 1) * ff_chunk], preferred_element_type=F32)
        u = jnp.maximum(u, 0.0)
        u = (u * u).astype(MXU_DTYPE)
        return acc + jnp.dot(u, wd_ref[c * ff_chunk:(c + 1) * ff_chunk, :], preferred_element_type=F32)

    def tail(r, h, acc):
        h = h + _rms(acc, g_ref[2:3, :])
        gate = _sigmoid(jnp.dot(h.astype(MXU_DTYPE), wg_ref[...], preferred_element_type=F32))
        emb = jnp.dot(p_ref[rows[r], :].astype(MXU_DTYPE), wp_ref[...], preferred_element_type=F32)
        o_ref[rows[r], :] = h + _rms(gate * emb, g_ref[3:4, :])

    half = n_ff // 2
    prev = None
    for r in range(n_sub):
        h, hn, acc = head(r)
        if prev is not None:
            ph, phn, pacc = prev
            for c in range(half, n_ff):
                pacc = ff(phn, pacc, c)
        for c in range(half):
            acc = ff(hn, acc, c)
        if prev is not None:
            tail(r - 1, ph, pacc)
        prev = (h, hn, acc)
    ph, phn, pacc = prev
    for c in range(half, n_ff):
        pacc = ff(phn, pacc, c)
    tail(n_sub - 1, ph, pacc)


def _post_mixer(h, y, p, gains4, w_out, w_up, w_down, w_gate, w_ple, tm=1024, sub_rows=256, ff_chunk=1024):
    m, d = h.shape
    k = y.shape[1]
    dp = p.shape[1]
    weights = [w.astype(MXU_DTYPE) for w in (w_out, w_up, w_down, w_gate, w_ple)]
    return pl.pallas_call(
        functools.partial(_post_mixer_kernel, ff_chunk=ff_chunk, sub_rows=sub_rows),
        grid=(m // tm,),
        in_specs=[pl.BlockSpec((tm, d), lambda i: (i, 0)),
                  pl.BlockSpec((tm, k), lambda i: (i, 0)),
                  pl.BlockSpec((tm, dp), lambda i: (i, 0)),
                  _const_spec((4, d))] + [_single_buffered(w.shape) for w in weights],
        out_specs=pl.BlockSpec((tm, d), lambda i: (i, 0)),
        out_shape=jax.ShapeDtypeStruct((m, d), F32),
        compiler_params=_params("parallel"),
        name="post_mixer",
    )(h, y, p, gains4, *weights)


def _tril_mask(n):
    r = lax.broadcasted_iota(jnp.int32, (n, n), 0)
    c = lax.broadcasted_iota(jnp.int32, (n, n), 1)
    return r >= c


def _mlstm_gates(pre, m_ref):
    L = pre.shape[0]
    lane = lax.broadcasted_iota(jnp.int32, (L, LANES), 1)
    t_idx = lax.broadcasted_iota(jnp.int32, (L, LANES), 0)
    val = jnp.where(lane < A_HEADS, pre, jnp.where(lane < 2 * A_HEADS, _log_sigmoid(pre), 0.0))
    bc = jnp.dot(_tril_mask(L).astype(F32), val, preferred_element_type=F32,
                 precision=lax.Precision.HIGHEST)
    b = pltpu.roll(bc, LANES - A_HEADS, 1)
    a = val - b
    cm = a
    step = 1
    while step < L:
        cm = jnp.maximum(cm, jnp.where(t_idx >= step, pltpu.roll(cm, step, 0), -jnp.inf))
        step *= 2
    m_st = m_ref[...]
    mx = jnp.maximum(cm, m_st)
    mx_last = mx[L - 1:L, :]
    m_ref[...] = b[L - 1:L, :] + mx_last
    return dict(
        mx=mx,
        a_t=a.T,
        wk_t=jnp.exp(a - mx_last).T,
        decay=jnp.exp(m_st - mx_last),
        w_inter=jnp.exp(m_st - mx) * A_DQK ** -0.5,
        e_negm=jnp.exp(-(b + mx)))


def _mlstm_heads(gt, q, k, v, o_gate, hg_ref, c_ref, y_ref, rows):
    L = q.shape[0]
    causal = _tril_mask(L)
    scale = A_DQK ** -0.5
    k_t = k.T
    ones_col = (lax.broadcasted_iota(jnp.int32, (L, A_DV), 1) == 0).astype(MXU_DTYPE)
    heads = range(A_HEADS)
    qs = [q[:, hd * A_DQK:(hd + 1) * A_DQK].astype(MXU_DTYPE) for hd in heads]
    kts = [k_t[hd * A_DQK:(hd + 1) * A_DQK, :] for hd in heads]
    vs = [v[:, hd * A_DV:(hd + 1) * A_DV].astype(MXU_DTYPE) for hd in heads]
    cs = [c_ref[hd] for hd in heads]
    dmats = [jnp.exp(jnp.where(causal, gt["a_t"][hd:hd + 1, :] - gt["mx"][:, hd:hd + 1], -jnp.inf))
             for hd in heads]
    ss = [_mm(qs[hd], kts[hd]) * scale * dmats[hd] for hd in heads]
    inters = [_mm(qs[hd], cs[hd]) for hd in heads]
    for hd in heads:
        v_ext = jnp.concatenate([vs[hd], ones_col], axis=1)
        c_ref[hd] = (gt["decay"][:, hd:hd + 1] * cs[hd]
                     + _mm(kts[hd] * gt["wk_t"][hd:hd + 1, :], v_ext))
    wis = [gt["w_inter"][:, hd:hd + 1] for hd in heads]
    nums = [_mm(ss[hd], vs[hd]) + wis[hd] * inters[hd][:, :A_DV] for hd in heads]
    dens = [jnp.sum(ss[hd], axis=1, keepdims=True) + wis[hd] * inters[hd][:, A_DV:A_DV + 1] for hd in heads]
    outs = [nums[hd] * (1.0 / jnp.maximum(jnp.abs(dens[hd]), gt["e_negm"][:, hd:hd + 1])) for hd in heads]
    for hd in heads:
        sl = slice(hd * A_DV, (hd + 1) * A_DV)
        y_ref[rows, sl] = (o_gate[:, sl] * _rms(outs[hd], hg_ref[:, sl])).astype(y_ref.dtype)


def _mlstm_kernel(h_ref, g_ref, w_ref, bias_ref, hg_ref, y_ref, c_ref, m_ref):
    tm, d = y_ref.shape
    qk_w = A_HEADS * A_DQK

    @pl.when(pl.program_id(1) == 0)
    def _():
        c_ref[...] = jnp.zeros(c_ref.shape, F32)
        m_ref[...] = jnp.zeros(m_ref.shape, F32)

    hn = _rms(h_ref[...], g_ref[...]).astype(MXU_DTYPE)
    proj = lambda lo, hi: jnp.dot(hn, w_ref[:, lo:hi], preferred_element_type=F32)
    pre = proj(2 * qk_w + 2 * d, 2 * qk_w + 2 * d + LANES) + bias_ref[...]
    q = proj(0, qk_w)
    k = proj(qk_w, 2 * qk_w)
    chunks = [slice(c * CHUNK, (c + 1) * CHUNK) for c in range(tm // CHUNK)]
    gt = _mlstm_gates(pre[chunks[0], :], m_ref)
    v = proj(2 * qk_w, 2 * qk_w + d)
    o_gate = _sigmoid(proj(2 * qk_w + d, 2 * qk_w + 2 * d))
    for c, rows in enumerate(chunks):
        if c > 0:
            gt = _mlstm_gates(pre[rows, :], m_ref)
        _mlstm_heads(gt, q[rows, :], k[rows, :], v[rows, :], o_gate[rows, :], hg_ref, c_ref, y_ref, rows)


def _mlstm_mixer(h, gain, w_in, ig_bias, fg_bias, head_gain, bsz, seq, tm=2 * CHUNK):
    m, dm = h.shape
    d = A_HEADS * A_DV
    nt = seq // tm
    pad = LANES - 2 * A_HEADS
    w_in = jnp.pad(w_in, ((0, 0), (0, pad)))
    bias = jnp.pad(jnp.concatenate([ig_bias, fg_bias]), (0, pad)).reshape(1, LANES)
    return pl.pallas_call(
        _mlstm_kernel,
        grid=(bsz, nt),
        in_specs=[pl.BlockSpec((tm, dm), lambda b, t: (b * nt + t, 0)),
                  _const_spec((1, dm)),
                  _single_buffered(w_in.shape),
                  _const_spec((1, LANES)),
                  _const_spec((1, d))],
        out_specs=pl.BlockSpec((tm, d), lambda b, t: (b * nt + t, 0)),
        out_shape=jax.ShapeDtypeStruct((m, d), MXU_DTYPE),
        scratch_shapes=[pltpu.VMEM((A_HEADS, A_DQK, 2 * A_DV), F32),
                        pltpu.VMEM((1, LANES), F32)],
        compiler_params=_params("parallel", "arbitrary"),
        name="mlstm_mixer",
    )(h, gain.reshape(1, dm), w_in.astype(MXU_DTYPE), bias, head_gain.reshape(1, d))


def _block_ref_rows(g, w):
    L, D = g.shape
    if 2 * w >= SUBLANES:
        g3 = g.reshape(L // (2 * w), 2 * w, D)
        return jnp.broadcast_to(g3[:, w - 1:w, :], g3.shape).reshape(L, D)
    g3 = g.reshape(L // SUBLANES, SUBLANES, D)
    sub = lax.broadcasted_iota(jnp.int32, g3.shape, 1)
    out = None
    for start in range(0, SUBLANES, 2 * w):
        r = jnp.broadcast_to(g3[:, start + w - 1:start + w, :], g3.shape)
        out = r if out is None else jnp.where(sub >= start, r, out)
    return out.reshape(L, D)


def _split3(x):
    hi = x.astype(MXU_DTYPE)
    r1 = x - hi.astype(F32)
    mid = r1.astype(MXU_DTYPE)
    lo = (r1 - mid.astype(F32)).astype(MXU_DTYPE)
    return jnp.concatenate([hi, mid, lo], axis=1)


def _dot3(mat, x3, n):
    r = jnp.dot(mat, x3, preferred_element_type=F32)
    return r[:, 0:n] + r[:, n:2 * n] + r[:, 2 * n:3 * n]


def _hgrn2_attn(q, log_f, k):
    L, D = q.shape
    dk = B_DK
    row = lax.broadcasted_iota(jnp.int32, (L, L), 0)
    col = lax.broadcasted_iota(jnp.int32, (L, L), 1)
    trow = lax.broadcasted_iota(jnp.int32, (L, 1), 0)
    diag = row == col
    g = _dot3((row >= col).astype(MXU_DTYPE), _split3(log_f), D)

    attn = []
    for hd in range(B_HEADS):
        sl = slice(hd * dk, (hd + 1) * dk)
        attn.append(jnp.where(diag, jnp.sum(q[:, sl] * k[:, sl], axis=1, keepdims=True), 0.0))
    w = L // 2
    while w >= 1:
        upper = (trow % (2 * w)) >= w
        ref = _block_ref_rows(g, w)
        x = jnp.exp(jnp.where(upper, g - ref, ref - g))
        z = (jnp.where(upper, q, k) * x).astype(MXU_DTYPE)
        mask = ((row // (2 * w)) == (col // (2 * w))) & ((row % (2 * w)) >= w) & ((col % (2 * w)) < w)
        for hd in range(B_HEADS):
            zh = z[:, hd * dk:(hd + 1) * dk]
            attn[hd] = jnp.where(mask, _mm_nt(zh, zh), attn[hd])
        w //= 2
    return attn, g


def _hgrn2_tail(attn, g, q, k, iv, silu_g, hg_ref, st_ref, y_ref, rows):
    L = q.shape[0]
    dk = B_DK
    g_last = g[L - 1:L, :]
    q_dec = (q * jnp.exp(g)).astype(MXU_DTYPE)
    k_dec = (k * jnp.exp(g_last - g)).astype(MXU_DTYPE)
    s_dec = jnp.exp(g_last)
    heads = range(B_HEADS)
    sls = [slice(hd * dk, (hd + 1) * dk) for hd in heads]
    vs = [iv[:, sl] for sl in sls]
    sts = [st_ref[hd] for hd in heads]
    outs = [_mm(attn[hd], vs[hd]) + _mm_nt(q_dec[:, sls[hd]], sts[hd]) for hd in heads]
    for hd in heads:
        st_ref[hd] = s_dec[:, sls[hd]] * sts[hd] + _mm(vs[hd].T, k_dec[:, sls[hd]])
    for hd in heads:
        y = _rms(outs[hd], hg_ref[:, sls[hd]]) * silu_g[:, sls[hd]]
        y_ref[rows, sls[hd]] = y.astype(y_ref.dtype)


def _hgrn2_kernel(h_ref, g_ref, w_ref, lbp_ref, hg_ref, y_ref, st_ref, *, layer):
    tm, d = y_ref.shape

    @pl.when(pl.program_id(1) == 0)
    def _():
        st_ref[...] = jnp.zeros(st_ref.shape, F32)

    hn = _rms(h_ref[...], g_ref[...]).astype(MXU_DTYPE)
    proj = lambda j: jnp.dot(hn, w_ref[:, j * d:(j + 1) * d], preferred_element_type=F32)
    q = proj(0)
    log_f, k = _hgrn2_gates(proj(1), lbp_ref[...], layer)
    chunks = [slice(c * CHUNK, (c + 1) * CHUNK) for c in range(tm // CHUNK)]
    attn, g = _hgrn2_attn(q[chunks[0], :], log_f[chunks[0], :], k[chunks[0], :])
    iv = proj(2)
    gz = proj(3)
    silu_g = gz * _sigmoid(gz)
    for c, rows in enumerate(chunks):
        if c > 0:
            attn, g = _hgrn2_attn(q[rows, :], log_f[rows, :], k[rows, :])
        _hgrn2_tail(attn, g, q[rows, :], k[rows, :], iv[rows, :], silu_g[rows, :], hg_ref, st_ref, y_ref, rows)


def _hgrn2_mixer(h, gain, w_in, lower_bound, head_gain, layer, bsz, seq, tm=4 * CHUNK):
    m, dm = h.shape
    d = B_HEADS * B_DK
    nt = seq // tm
    return pl.pallas_call(
        functools.partial(_hgrn2_kernel, layer=layer),
        grid=(bsz, nt),
        in_specs=[pl.BlockSpec((tm, dm), lambda b, t: (b * nt + t, 0)),
                  _const_spec((1, dm)),
                  _single_buffered(w_in.shape),
                  _const_spec(lower_bound.shape),
                  _const_spec((1, d))],
        out_specs=pl.BlockSpec((tm, d), lambda b, t: (b * nt + t, 0)),
        out_shape=jax.ShapeDtypeStruct((m, d), MXU_DTYPE),
        scratch_shapes=[pltpu.VMEM((B_HEADS, B_DK, B_DK), F32)],
        compiler_params=_params("parallel", "arbitrary"),
        name="hgrn2_mixer",
    )(h, gain.reshape(1, dm), w_in.astype(MXU_DTYPE), lower_bound, head_gain.reshape(1, d))


def _gmlp_kernel(h_ref, g_ref, w_ref, lng_ref, lnb_ref, ws_ref, bs_ref, y_ref):
    tm = h_ref.shape[0]
    half = y_ref.shape[1]
    gw = half // C_GROUPS
    hn = _rms(h_ref[...], g_ref[...]).astype(MXU_DTYPE)
    v = _gelu_tanh(jnp.dot(hn, w_ref[:, half:], preferred_element_type=F32))
    mu = jnp.mean(v, axis=-1, keepdims=True)
    vc = v - mu
    var = jnp.mean(vc * vc, axis=-1, keepdims=True)
    vn = (vc * lax.rsqrt(var + NORM_EPS) * lng_ref[...] + lnb_ref[...]).astype(MXU_DTYPE)
    causal = _tril_mask(CHUNK)
    for g in range(C_GROUPS):
        sl = slice(g * gw, (g + 1) * gw)
        u = _gelu_tanh(jnp.dot(hn, w_ref[:, sl], preferred_element_type=F32))
        w = jnp.where(causal, ws_ref[g], 0.0).astype(MXU_DTYPE)
        for c in range(tm // CHUNK):
            rows = slice(c * CHUNK, (c + 1) * CHUNK)
            vm = jnp.dot(w, vn[rows, sl], preferred_element_type=F32) + bs_ref[:, g:g + 1]
            y_ref[rows, sl] = (u[rows, :] * vm).astype(y_ref.dtype)


def _gmlp_mixer(h, gain, w_in, ln_gain, ln_bias, w_s, b_s, tm=512):
    m, d = h.shape
    half = w_in.shape[1] // 2
    return pl.pallas_call(
        _gmlp_kernel,
        grid=(m // tm,),
        in_specs=[pl.BlockSpec((tm, d), lambda i: (i, 0)),
                  _const_spec((1, d)),
                  _single_buffered(w_in.shape),
                  _const_spec((1, half)),
                  _const_spec((1, half)),
                  _const_spec(w_s.shape),
                  _const_spec((CHUNK, C_GROUPS))],
        out_specs=pl.BlockSpec((tm, half), lambda i: (i, 0)),
        out_shape=jax.ShapeDtypeStruct((m, half), MXU_DTYPE),
        compiler_params=_params("parallel"),
        name="gmlp_mixer",
    )(h, gain.reshape(1, d), w_in.astype(MXU_DTYPE), ln_gain.reshape(1, half), ln_bias.reshape(1, half),
      w_s, b_s.T)


def _rglru_kernel(h_ref, g_ref, w_ref, cw_ref, vec_ref, wa_ref, wx_ref, y_ref, xpad_ref, hc_ref):
    T, d = y_ref.shape
    halo = xpad_ref.shape[0] - T

    @pl.when(pl.program_id(1) == 0)
    def _():
        xpad_ref[0:halo, :] = jnp.zeros((halo, d), F32)
        hc_ref[...] = jnp.zeros(hc_ref.shape, F32)

    hn = _rms(h_ref[...], g_ref[...]).astype(MXU_DTYPE)
    xb = jnp.dot(hn, w_ref[:, d:], preferred_element_type=F32)
    xpad_ref[halo:, :] = xb
    xc = xb * cw_ref[D_CONV - 1:D_CONV, :] + vec_ref[0:1, :]
    for j in range(1, D_CONV):
        shifted = xpad_ref[halo - j:halo - j + T, :]
        xc = xc + shifted * cw_ref[D_CONV - 1 - j:D_CONV - j, :]
    xpad_ref[0:halo, :] = xb[T - halo:T, :]

    bw = d // D_BLOCKS
    r_parts, i_parts = [], []
    for n in range(D_BLOCKS):
        xn = xc[:, n * bw:(n + 1) * bw]
        r_parts.append(_mm(xn, wa_ref[n]))
        i_parts.append(_mm(xn, wx_ref[n]))
    gate = _gelu_tanh(jnp.dot(hn, w_ref[:, :d], preferred_element_type=F32))
    r = _sigmoid(jnp.concatenate(r_parts, axis=1) + vec_ref[1:2, :])
    ig = _sigmoid(jnp.concatenate(i_parts, axis=1) + vec_ref[2:3, :])
    nlam = -vec_ref[3:4, :]
    softplus = jnp.maximum(nlam, 0.0) + jnp.log1p(jnp.exp(-jnp.abs(nlam)))
    a = jnp.exp(-RG_C * r * softplus)
    om = 1.0 - a * a
    b = (om * lax.rsqrt(jnp.maximum(om, SQRT_FLOOR))) * (ig * xc)

    groups = T // SUBLANES
    a = a.reshape(groups, SUBLANES, d)
    b = b.reshape(groups, SUBLANES, d)
    sub = lax.broadcasted_iota(jnp.int32, a.shape, 1)
    step = 1
    while step < SUBLANES:
        keep = sub >= step
        a_prev = jnp.where(keep, pltpu.roll(a, step, 1), 1.0)
        b_prev = jnp.where(keep, pltpu.roll(b, step, 1), 0.0)
        b = a * b_prev + b
        a = a * a_prev
        step *= 2
    carry = hc_ref[...]
    rows = []
    for g in range(groups):
        hg = b[g] + a[g] * carry
        carry = hg[SUBLANES - 1:SUBLANES, :]
        rows.append(hg)
    hc_ref[...] = carry
    y_ref[...] = (jnp.concatenate(rows, axis=0) * gate).astype(y_ref.dtype)


def _rglru_mixer(h, gain, w_in, conv_w, conv_b, w_a, b_a, w_x, b_x, lam, bsz, seq, tt=256):
    m, dm = h.shape
    d = w_in.shape[1] // 2
    nt = seq // tt
    row = lambda b, t: b * nt + t
    vecs = jnp.stack([conv_b, b_a, b_x, lam])
    return pl.pallas_call(
        _rglru_kernel,
        grid=(bsz, nt),
        in_specs=[pl.BlockSpec((tt, dm), lambda b, t: (row(b, t), 0)),
                  _const_spec((1, dm)),
                  _single_buffered(w_in.shape),
                  _const_spec(conv_w.shape),
                  _const_spec(vecs.shape),
                  _const_spec(w_a.shape),
                  _const_spec(w_x.shape)],
        out_specs=pl.BlockSpec((tt, d), lambda b, t: (row(b, t), 0)),
        out_shape=jax.ShapeDtypeStruct((m, d), MXU_DTYPE),
        scratch_shapes=[pltpu.VMEM((SUBLANES + tt, d), F32), pltpu.VMEM((1, d), F32)],
        compiler_params=_params("parallel", "arbitrary"),
        name="rglru_mixer",
    )(h, gain.reshape(1, dm), w_in.astype(MXU_DTYPE), conv_w, vecs, w_a.astype(MXU_DTYPE),
      w_x.astype(MXU_DTYPE))


def kernel(x, p, norm_gains, mlp_w_up, mlp_w_down, ple_w_up, ple_w_gate, a_w_in, a_ig_bias, a_fg_bias, a_head_gain, a_w_out, b_w_in, b_lower_bound, b_head_gain, b_w_out, c_w_in, c_ln_gain, c_ln_bias, c_spatial_w, c_spatial_b, c_w_out, d_w_in, d_conv_w, d_conv_b, d_w_a, d_b_a, d_w_x, d_b_x, d_lambda, d_w_out):
    bsz, seq, d = x.shape
    depth = norm_gains.shape[0]
    n_mixers = 4
    h = x.reshape(bsz * seq, d)
    for i in range(depth):
        kind, j = i % n_mixers, i // n_mixers
        g = norm_gains[i]
        if kind == 0:
            y = _mlstm_mixer(h, g[0], a_w_in[j], a_ig_bias[j], a_fg_bias[j], a_head_gain[j], bsz, seq)
            w_out = a_w_out[j]
        elif kind == 1:
            y = _hgrn2_mixer(h, g[0], b_w_in[j], b_lower_bound, b_head_gain[j], i, bsz, seq)
            w_out = b_w_out[j]
        elif kind == 2:
            y = _gmlp_mixer(h, g[0], c_w_in[j], c_ln_gain[j], c_ln_bias[j], c_spatial_w[j], c_spatial_b[j])
            w_out = c_w_out[j]
        else:
            y = _rglru_mixer(h, g[0], d_w_in[j], d_conv_w[j], d_conv_b[j], d_w_a[j], d_b_a[j], d_w_x[j],
                             d_b_x[j], d_lambda[j], bsz, seq)
            w_out = d_w_out[j]
        h = _post_mixer(h, y, p[i].reshape(bsz * seq, -1), g[1:5], w_out, mlp_w_up[i], mlp_w_down[i],
                        ple_w_gate[i], ple_w_up[i])
    return h.reshape(bsz, seq, d)
```

```python
import functools

import jax
import jax.numpy as jnp
from jax import lax
from jax.experimental import pallas as pl
from jax.experimental.pallas import tpu as pltpu

F32 = jnp.float32
MXU_DTYPE = jnp.bfloat16
NORM_EPS = 1e-6
RG_C = 8.0

LANES = 128
SUBLANES = 8
SQRT_FLOOR = 1e-30
CHUNK = 128
VMEM_LIMIT = 63 * 1024 * 1024

A_HEADS, A_DQK, A_DV = 8, 64, 128
B_HEADS, B_DK = 8, 128
C_GROUPS = 8
D_BLOCKS, D_CONV = 4, 4


def _params(*sem):
    return pltpu.CompilerParams(dimension_semantics=sem, vmem_limit_bytes=VMEM_LIMIT)


def _mm(a, b):
    return jnp.dot(a.astype(MXU_DTYPE), b.astype(MXU_DTYPE), preferred_element_type=F32)


def _mm_nt(a, b):
    return lax.dot_general(a.astype(MXU_DTYPE), b.astype(MXU_DTYPE),
                           (((1,), (1,)), ((), ())), preferred_element_type=F32)


def _rms(x, gain):
    return x * lax.rsqrt(jnp.mean(x * x, axis=-1, keepdims=True) + NORM_EPS) * gain


def _sigmoid(x):
    return 0.5 * jnp.tanh(0.5 * x) + 0.5


def _log_sigmoid(x):
    return jnp.minimum(x, 0.0) - jnp.log(1.0 + jnp.exp(-jnp.abs(x)))


_GELU_K = 0.7978845608028654


def _gelu_tanh(x):
    hx = 0.5 * x
    return hx + hx * jnp.tanh(x * (_GELU_K + (_GELU_K * 0.044715) * (x * x)))


def _hgrn2_gates(fz, lbp, layer):
    e = jnp.exp(lbp - jnp.max(lbp, axis=0, keepdims=True))
    sm = e / jnp.sum(e, axis=0, keepdims=True)
    cs = sm[0:1, :]
    first = cs
    for r in range(1, layer + 1):
        cs = cs + sm[r:r + 1, :]
    lb = cs - first
    t = jnp.exp(-jnp.abs(fz))
    r = 1.0 / (1.0 + t)
    sig_abs, sig_nabs = r, t * r
    pos = fz >= 0.0
    sig = jnp.where(pos, sig_abs, sig_nabs)
    sig_neg = jnp.where(pos, sig_nabs, sig_abs)
    return jnp.log(lb + (1.0 - lb) * sig), (1.0 - lb) * sig_neg


def _const_spec(shape):
    nd = len(shape)
    return pl.BlockSpec(shape, lambda *_: (0,) * nd)


def _single_buffered(shape):
    nd = len(shape)
    return pl.BlockSpec(shape, lambda *_: (0,) * nd, pipeline_mode=pl.Buffered(1))


def _post_mixer_kernel(h_ref, y_ref, p_ref, g_ref, wo_ref, wu_ref, wd_ref, wg_ref, wp_ref, o_ref,
                       *, ff_chunk, sub_rows):
    n_ff = wu_ref.shape[1] // ff_chunk
    n_sub = h_ref.shape[0] // sub_rows
    rows = [slice(r * sub_rows, (r + 1) * sub_rows) for r in range(n_sub)]

    def head(r):
        h = h_ref[rows[r], :] + _rms(jnp.dot(y_ref[rows[r], :], wo_ref[...], preferred_element_type=F32),
                                     g_ref[0:1, :])
        return h, _rms(h, g_ref[1:2, :]).astype(MXU_DTYPE), jnp.zeros(h.shape, F32)

    def ff(hn, acc, c):
        u = jnp.dot(hn, wu_ref[:, c * ff_chunk:(c + 1) * ff_chunk], preferred_element_type=F32)
        u = jnp.maximum(u, 0.0)
        u = (u * u).astype(MXU_DTYPE)
        return acc + jnp.dot(u, wd_ref[c * ff_chunk:(c + 1) * ff_chunk, :], preferred_element_type=F32)

    def tail(r, h, acc):
        h = h + _rms(acc, g_ref[2:3, :])
        gate = _sigmoid(jnp.dot(h.astype(MXU_DTYPE), wg_ref[...], preferred_element_type=F32))
        emb = jnp.dot(p_ref[rows[r], :].astype(MXU_DTYPE), wp_ref[...], preferred_element_type=F32)
        o_ref[rows[r], :] = h + _rms(gate * emb, g_ref[3:4, :])

    half = n_ff // 2
    prev = None
    for r in range(n_sub):
        h, hn, acc = head(r)
        if prev is not None:
            ph, phn, pacc = prev
            for c in range(half, n_ff):
                pacc = ff(phn, pacc, c)
        for c in range(half):
            acc = ff(hn, acc, c)
        if prev is not None:
            tail(r - 1, ph, pacc)
        prev = (h, hn, acc)
    ph, phn, pacc = prev
    for c in range(half, n_ff):
        pacc = ff(phn, pacc, c)
    tail(n_sub - 1, ph, pacc)


def _post_mixer(h, y, p, gains4, w_out, w_up, w_down, w_gate, w_ple, tm=1024, sub_rows=256, ff_chunk=1024):
    m, d = h.shape
    k = y.shape[1]
    dp = p.shape[1]
    weights = [w.astype(MXU_DTYPE) for w in (w_out, w_up, w_down, w_gate, w_ple)]
    return pl.pallas_call(
        functools.partial(_post_mixer_kernel, ff_chunk=ff_chunk, sub_rows=sub_rows),
        grid=(m // tm,),
        in_specs=[pl.BlockSpec((tm, d), lambda i: (i, 0)),
                  pl.BlockSpec((tm, k), lambda i: (i, 0)),
                  pl.BlockSpec((tm, dp), lambda i: (i, 0)),
                  _const_spec((4, d))] + [_single_buffered(w.shape) for w in weights],
        out_specs=pl.BlockSpec((tm, d), lambda i: (i, 0)),
        out_shape=jax.ShapeDtypeStruct((m, d), F32),
        compiler_params=_params("parallel"),
        name="post_mixer",
    )(h, y, p, gains4, *weights)


def _tril_mask(n):
    r = lax.broadcasted_iota(jnp.int32, (n, n), 0)
    c = lax.broadcasted_iota(jnp.int32, (n, n), 1)
    return r >= c


def _mlstm_gates(pre, m_ref):
    L = pre.shape[0]
    lane = lax.broadcasted_iota(jnp.int32, (L, LANES), 1)
    t_idx = lax.broadcasted_iota(jnp.int32, (L, LANES), 0)
    val = jnp.where(lane < A_HEADS, pre, jnp.where(lane < 2 * A_HEADS, _log_sigmoid(pre), 0.0))
    bc = jnp.dot(_tril_mask(L).astype(F32), val, preferred_element_type=F32,
                 precision=lax.Precision.HIGHEST)
    b = pltpu.roll(bc, LANES - A_HEADS, 1)
    a = val - b
    cm = a
    step = 1
    while step < L:
        cm = jnp.maximum(cm, jnp.where(t_idx >= step, pltpu.roll(cm, step, 0), -jnp.inf))
        step *= 2
    m_st = m_ref[...]
    mx = jnp.maximum(cm, m_st)
    mx_last = mx[L - 1:L, :]
    m_ref[...] = b[L - 1:L, :] + mx_last
    return dict(
        mx=mx,
        a_t=a.T,
        wk_t=jnp.exp(a - mx_last).T,
        decay=jnp.exp(m_st - mx_last),
        w_inter=jnp.exp(m_st - mx) * A_DQK ** -0.5,
        e_negm=jnp.exp(-(b + mx)))


def _mlstm_heads(gt, q, k, v, o_gate, hg_ref, c_ref, y_ref, rows):
    L = q.shape[0]
    causal = _tril_mask(L)
    scale = A_DQK ** -0.5
    k_t = k.T
    ones_col = (lax.broadcasted_iota(jnp.int32, (L, A_DV), 1) == 0).astype(MXU_DTYPE)
    heads = range(A_HEADS)
    qs = [q[:, hd * A_DQK:(hd + 1) * A_DQK].astype(MXU_DTYPE) for hd in heads]
    kts = [k_t[hd * A_DQK:(hd + 1) * A_DQK, :] for hd in heads]
    vs = [v[:, hd * A_DV:(hd + 1) * A_DV].astype(MXU_DTYPE) for hd in heads]
    cs = [c_ref[hd] for hd in heads]
    dmats = [jnp.exp(jnp.where(causal, gt["a_t"][hd:hd + 1, :] - gt["mx"][:, hd:hd + 1], -jnp.inf))
             for hd in heads]
    ss = [_mm(qs[hd], kts[hd]) * scale * dmats[hd] for hd in heads]
    inters = [_mm(qs[hd], cs[hd]) for hd in heads]
    for hd in heads:
        v_ext = jnp.concatenate([vs[hd], ones_col], axis=1)
        c_ref[hd] = (gt["decay"][:, hd:hd + 1] * cs[hd]
                     + _mm(kts[hd] * gt["wk_t"][hd:hd + 1, :], v_ext))
    wis = [gt["w_inter"][:, hd:hd + 1] for hd in heads]
    nums = [_mm(ss[hd], vs[hd]) + wis[hd] * inters[hd][:, :A_DV] for hd in heads]
    dens = [jnp.sum(ss[hd], axis=1, keepdims=True) + wis[hd] * inters[hd][:, A_DV:A_DV + 1] for hd in heads]
    outs = [nums[hd] * (1.0 / jnp.maximum(jnp.abs(dens[hd]), gt["e_negm"][:, hd:hd + 1])) for hd in heads]
    for hd in heads:
        sl = slice(hd * A_DV, (hd + 1) * A_DV)
        y_ref[rows, sl] = (o_gate[:, sl] * _rms(outs[hd], hg_ref[:, sl])).astype(y_ref.dtype)


def _mlstm_kernel(h_ref, g_ref, w_ref, bias_ref, hg_ref, y_ref, c_ref, m_ref):
    tm, d = y_ref.shape
    qk_w = A_HEADS * A_DQK

    @pl.when(pl.program_id(1) == 0)
    def _():
        c_ref[...] = jnp.zeros(c_ref.shape, F32)
        m_ref[...] = jnp.zeros(m_ref.shape, F32)

    hn = _rms(h_ref[...], g_ref[...]).astype(MXU_DTYPE)
    proj = lambda lo, hi: jnp.dot(hn, w_ref[:, lo:hi], preferred_element_type=F32)
    pre = proj(2 * qk_w + 2 * d, 2 * qk_w + 2 * d + LANES) + bias_ref[...]
    q = proj(0, qk_w)
    k = proj(qk_w, 2 * qk_w)
    chunks = [slice(c * CHUNK, (c + 1) * CHUNK) for c in range(tm // CHUNK)]
    gt = _mlstm_gates(pre[chunks[0], :], m_ref)
    v = proj(2 * qk_w, 2 * qk_w + d)
    o_gate = _sigmoid(proj(2 * qk_w + d, 2 * qk_w + 2 * d))
    for c, rows in enumerate(chunks):
        if c > 0:
            gt = _mlstm_gates(pre[rows, :], m_ref)
        _mlstm_heads(gt, q[rows, :], k[rows, :], v[rows, :], o_gate[rows, :], hg_ref, c_ref, y_ref, rows)


def _mlstm_mixer(h, gain, w_in, ig_bias, fg_bias, head_gain, bsz, seq, tm=2 * CHUNK):
    m, dm = h.shape
    d = A_HEADS * A_DV
    nt = seq // tm
    pad = LANES - 2 * A_HEADS
    w_in = jnp.pad(w_in, ((0, 0), (0, pad)))
    bias = jnp.pad(jnp.concatenate([ig_bias, fg_bias]), (0, pad)).reshape(1, LANES)
    return pl.pallas_call(
        _mlstm_kernel,
        grid=(bsz, nt),
        in_specs=[pl.BlockSpec((tm, dm), lambda b, t: (b * nt + t, 0)),
                  _const_spec((1, dm)),
                  _single_buffered(w_in.shape),
                  _const_spec((1, LANES)),
                  _const_spec((1, d))],
        out_specs=pl.BlockSpec((tm, d), lambda b, t: (b * nt + t, 0)),
        out_shape=jax.ShapeDtypeStruct((m, d), MXU_DTYPE),
        scratch_shapes=[pltpu.VMEM((A_HEADS, A_DQK, 2 * A_DV), F32),
                        pltpu.VMEM((1, LANES), F32)],
        compiler_params=_params("parallel", "arbitrary"),
        name="mlstm_mixer",
    )(h, gain.reshape(1, dm), w_in.astype(MXU_DTYPE), bias, head_gain.reshape(1, d))


def _block_ref_rows(g, w):
    L, D = g.shape
    if 2 * w >= SUBLANES:
        g3 = g.reshape(L // (2 * w), 2 * w, D)
        return jnp.broadcast_to(g3[:, w - 1:w, :], g3.shape).reshape(L, D)
    g3 = g.reshape(L // SUBLANES, SUBLANES, D)
    sub = lax.broadcasted_iota(jnp.int32, g3.shape, 1)
    out = None
    for start in range(0, SUBLANES, 2 * w):
        r = jnp.broadcast_to(g3[:, start + w - 1:start + w, :], g3.shape)
        out = r if out is None else jnp.where(sub >= start, r, out)
    return out.reshape(L, D)


def _split3(x):
    hi = x.astype(MXU_DTYPE)
    r1 = x - hi.astype(F32)
    mid = r1.astype(MXU_DTYPE)
    lo = (r1 - mid.astype(F32)).astype(MXU_DTYPE)
    return jnp.concatenate([hi, mid, lo], axis=1)


def _dot3(mat, x3, n):
    r = jnp.dot(mat, x3, preferred_element_type=F32)
    return r[:, 0:n] + r[:, n:2 * n] + r[:, 2 * n:3 * n]


def _hgrn2_attn(q, log_f, k, filler):
    L, D = q.shape
    dk = B_DK
    row = lax.broadcasted_iota(jnp.int32, (L, L), 0)
    col = lax.broadcasted_iota(jnp.int32, (L, L), 1)
    trow = lax.broadcasted_iota(jnp.int32, (L, 1), 0)
    diag = row == col
    g = _dot3((row >= col).astype(MXU_DTYPE), _split3(log_f), D)

    attn = []
    for hd in range(B_HEADS):
        sl = slice(hd * dk, (hd + 1) * dk)
        attn.append(jnp.where(diag, jnp.sum(q[:, sl] * k[:, sl], axis=1, keepdims=True), 0.0))
    w = L // 2
    level = 0
    while w >= 1:
        upper = (trow % (2 * w)) >= w
        mask = ((row // (2 * w)) == (col // (2 * w))) & ((row % (2 * w)) >= w) & ((col % (2 * w)) < w)
        if level % 2 == 0:
            next(filler, None)
        level += 1
        for hd in range(B_HEADS):
            sl = slice(hd * dk, (hd + 1) * dk)
            gh = g[:, sl]
            ref = _block_ref_rows(gh, w)
            x = jnp.exp(jnp.where(upper, gh - ref, ref - gh))
            zh = (jnp.where(upper, q[:, sl], k[:, sl]) * x).astype(MXU_DTYPE)
            attn[hd] = jnp.where(mask, _mm_nt(zh, zh), attn[hd])
        w //= 2
    return attn, g


def _hgrn2_tail(attn, g, q, k, iv, silu_g, hg_ref, st_ref, y_ref, rows):
    L = q.shape[0]
    dk = B_DK
    g_last = g[L - 1:L, :]
    q_dec = (q * jnp.exp(g)).astype(MXU_DTYPE)
    k_dec = (k * jnp.exp(g_last - g)).astype(MXU_DTYPE)
    s_dec = jnp.exp(g_last)
    heads = range(B_HEADS)
    sls = [slice(hd * dk, (hd + 1) * dk) for hd in heads]
    vs = [iv[:, sl] for sl in sls]
    sts = [st_ref[hd] for hd in heads]
    outs = [_mm(attn[hd], vs[hd]) + _mm_nt(q_dec[:, sls[hd]], sts[hd]) for hd in heads]
    for hd in heads:
        st_ref[hd] = s_dec[:, sls[hd]] * sts[hd] + _mm(vs[hd].T, k_dec[:, sls[hd]])
    for hd in heads:
        y = _rms(outs[hd], hg_ref[:, sls[hd]]) * silu_g[:, sls[hd]]
        y_ref[rows, sls[hd]] = y.astype(y_ref.dtype)


_Z_Q, _Z_LOGF, _Z_KEY, _Z_V, _Z_GATE = range(5)
PROJ_COLS = 256


def _hgrn2_project(hn, w_ref, lbp_ref, z_ref, slot, layer, col):
    d = z_ref.shape[-1]
    z = jnp.dot(hn, w_ref[:, col:col + PROJ_COLS], preferred_element_type=F32)
    group, c0 = divmod(col, d)
    cols = slice(c0, c0 + PROJ_COLS)
    if group == 0:
        z_ref[slot, _Z_Q, :, cols] = z
    elif group == 1:
        log_f, key = _hgrn2_gates(z, lbp_ref[:, cols], layer)
        z_ref[slot, _Z_LOGF, :, cols] = log_f
        z_ref[slot, _Z_KEY, :, cols] = key
    elif group == 2:
        z_ref[slot, _Z_V, :, cols] = z
    else:
        z_ref[slot, _Z_GATE, :, cols] = z * _sigmoid(z)


def _hgrn2_kernel(h0_ref, hn_ref, g_ref, w_ref, lbp_ref, hg_ref, y_ref, z_ref, st_ref, *, layer):
    tm, d = y_ref.shape
    step = pl.program_id(0) * pl.num_programs(1) + pl.program_id(1)
    slot = step % 2
    n_pieces = w_ref.shape[1] // PROJ_COLS

    @pl.when(step == 0)
    def _():
        hn0 = _rms(h0_ref[...], g_ref[...]).astype(MXU_DTYPE)
        for j in range(n_pieces):
            _hgrn2_project(hn0, w_ref, lbp_ref, z_ref, 0, layer, j * PROJ_COLS)

    @pl.when(pl.program_id(1) == 0)
    def _():
        st_ref[...] = jnp.zeros(st_ref.shape, F32)

    hn_next = _rms(hn_ref[...], g_ref[...]).astype(MXU_DTYPE)

    def pieces():
        for j in range(n_pieces):
            yield _hgrn2_project(hn_next, w_ref, lbp_ref, z_ref, 1 - slot, layer, j * PROJ_COLS)

    filler = pieces()
    for c in range(tm // CHUNK):
        rows = slice(c * CHUNK, (c + 1) * CHUNK)
        q = z_ref[slot, _Z_Q, rows, :]
        k = z_ref[slot, _Z_KEY, rows, :]
        attn, g = _hgrn2_attn(q, z_ref[slot, _Z_LOGF, rows, :], k, filler)
        _hgrn2_tail(attn, g, q, k, z_ref[slot, _Z_V, rows, :], z_ref[slot, _Z_GATE, rows, :],
                    hg_ref, st_ref, y_ref, rows)
    for _ in filler:
        pass


def _hgrn2_mixer(h, gain, w_in, lower_bound, head_gain, layer, bsz, seq, tm=4 * CHUNK):
    m, dm = h.shape
    d = B_HEADS * B_DK
    nt = seq // tm
    last = m // tm - 1
    return pl.pallas_call(
        functools.partial(_hgrn2_kernel, layer=layer),
        grid=(bsz, nt),
        in_specs=[pl.BlockSpec((tm, dm), lambda b, t: (0, 0)),
                  pl.BlockSpec((tm, dm), lambda b, t: (jnp.minimum(b * nt + t + 1, last), 0)),
                  _const_spec((1, dm)),
                  _single_buffered(w_in.shape),
                  _const_spec(lower_bound.shape),
                  _const_spec((1, d))],
        out_specs=pl.BlockSpec((tm, d), lambda b, t: (b * nt + t, 0)),
        out_shape=jax.ShapeDtypeStruct((m, d), MXU_DTYPE),
        scratch_shapes=[pltpu.VMEM((2, 5, tm, d), F32),
                        pltpu.VMEM((B_HEADS, B_DK, B_DK), F32)],
        compiler_params=_params("arbitrary", "arbitrary"),
        name="hgrn2_mixer",
    )(h, h, gain.reshape(1, dm), w_in.astype(MXU_DTYPE), lower_bound, head_gain.reshape(1, d))


def _gmlp_kernel(h_ref, g_ref, w_ref, lng_ref, lnb_ref, ws_ref, bs_ref, y_ref):
    tm = h_ref.shape[0]
    half = y_ref.shape[1]
    gw = half // C_GROUPS
    hn = _rms(h_ref[...], g_ref[...]).astype(MXU_DTYPE)
    v = _gelu_tanh(jnp.dot(hn, w_ref[:, half:], preferred_element_type=F32))
    mu = jnp.mean(v, axis=-1, keepdims=True)
    vc = v - mu
    var = jnp.mean(vc * vc, axis=-1, keepdims=True)
    vn = (vc * lax.rsqrt(var + NORM_EPS) * lng_ref[...] + lnb_ref[...]).astype(MXU_DTYPE)
    causal = _tril_mask(CHUNK)
    for g in range(C_GROUPS):
        sl = slice(g * gw, (g + 1) * gw)
        u = _gelu_tanh(jnp.dot(hn, w_ref[:, sl], preferred_element_type=F32))
        w = jnp.where(causal, ws_ref[g], 0.0).astype(MXU_DTYPE)
        for c in range(tm // CHUNK):
            rows = slice(c * CHUNK, (c + 1) * CHUNK)
            vm = jnp.dot(w, vn[rows, sl], preferred_element_type=F32) + bs_ref[:, g:g + 1]
            y_ref[rows, sl] = (u[rows, :] * vm).astype(y_ref.dtype)


def _gmlp_mixer(h, gain, w_in, ln_gain, ln_bias, w_s, b_s, tm=512):
    m, d = h.shape
    half = w_in.shape[1] // 2
    return pl.pallas_call(
        _gmlp_kernel,
        grid=(m // tm,),
        in_specs=[pl.BlockSpec((tm, d), lambda i: (i, 0)),
                  _const_spec((1, d)),
                  _single_buffered(w_in.shape),
                  _const_spec((1, half)),
                  _const_spec((1, half)),
                  _const_spec(w_s.shape),
                  _const_spec((CHUNK, C_GROUPS))],
        out_specs=pl.BlockSpec((tm, half), lambda i: (i, 0)),
        out_shape=jax.ShapeDtypeStruct((m, half), MXU_DTYPE),
        compiler_params=_params("parallel"),
        name="gmlp_mixer",
    )(h, gain.reshape(1, d), w_in.astype(MXU_DTYPE), ln_gain.reshape(1, half), ln_bias.reshape(1, half),
      w_s, b_s.T)


def _rglru_kernel(h_ref, g_ref, w_ref, cw_ref, vec_ref, wa_ref, wx_ref, y_ref, xpad_ref, hc_ref):
    T, d = y_ref.shape
    halo = xpad_ref.shape[0] - T

    @pl.when(pl.program_id(1) == 0)
    def _():
        xpad_ref[0:halo, :] = jnp.zeros((halo, d), F32)
        hc_ref[...] = jnp.zeros(hc_ref.shape, F32)

    hn = _rms(h_ref[...], g_ref[...]).astype(MXU_DTYPE)
    xb = jnp.dot(hn, w_ref[:, d:], preferred_element_type=F32)
    xpad_ref[halo:, :] = xb
    xc = xb * cw_ref[D_CONV - 1:D_CONV, :] + vec_ref[0:1, :]
    for j in range(1, D_CONV):
        shifted = xpad_ref[halo - j:halo - j + T, :]
        xc = xc + shifted * cw_ref[D_CONV - 1 - j:D_CONV - j, :]
    xpad_ref[0:halo, :] = xb[T - halo:T, :]

    bw = d // D_BLOCKS
    r_parts, i_parts = [], []
    for n in range(D_BLOCKS):
        xn = xc[:, n * bw:(n + 1) * bw]
        r_parts.append(_mm(xn, wa_ref[n]))
        i_parts.append(_mm(xn, wx_ref[n]))
    gate = _gelu_tanh(jnp.dot(hn, w_ref[:, :d], preferred_element_type=F32))
    r = _sigmoid(jnp.concatenate(r_parts, axis=1) + vec_ref[1:2, :])
    ig = _sigmoid(jnp.concatenate(i_parts, axis=1) + vec_ref[2:3, :])
    nlam = -vec_ref[3:4, :]
    softplus = jnp.maximum(nlam, 0.0) + jnp.log1p(jnp.exp(-jnp.abs(nlam)))
    a = jnp.exp(-RG_C * r * softplus)
    om = 1.0 - a * a
    b = (om * lax.rsqrt(jnp.maximum(om, SQRT_FLOOR))) * (ig * xc)

    groups = T // SUBLANES
    a = a.reshape(groups, SUBLANES, d)
    b = b.reshape(groups, SUBLANES, d)
    sub = lax.broadcasted_iota(jnp.int32, a.shape, 1)
    step = 1
    while step < SUBLANES:
        keep = sub >= step
        a_prev = jnp.where(keep, pltpu.roll(a, step, 1), 1.0)
        b_prev = jnp.where(keep, pltpu.roll(b, step, 1), 0.0)
        b = a * b_prev + b
        a = a * a_prev
        step *= 2
    carry = hc_ref[...]
    rows = []
    for g in range(groups):
        hg = b[g] + a[g] * carry
        carry = hg[SUBLANES - 1:SUBLANES, :]
        rows.append(hg)
    hc_ref[...] = carry
    y_ref[...] = (jnp.concatenate(rows, axis=0) * gate).astype(y_ref.dtype)


def _rglru_mixer(h, gain, w_in, conv_w, conv_b, w_a, b_a, w_x, b_x, lam, bsz, seq, tt=256):
    m, dm = h.shape
    d = w_in.shape[1] // 2
    nt = seq // tt
    row = lambda b, t: b * nt + t
    vecs = jnp.stack([conv_b, b_a, b_x, lam])
    return pl.pallas_call(
        _rglru_kernel,
        grid=(bsz, nt),
        in_specs=[pl.BlockSpec((tt, dm), lambda b, t: (row(b, t), 0)),
                  _const_spec((1, dm)),
                  _single_buffered(w_in.shape),
                  _const_spec(conv_w.shape),
                  _const_spec(vecs.shape),
                  _const_spec(w_a.shape),
                  _const_spec(w_x.shape)],
        out_specs=pl.BlockSpec((tt, d), lambda b, t: (row(b, t), 0)),
        out_shape=jax.ShapeDtypeStruct((m, d), MXU_DTYPE),
        scratch_shapes=[pltpu.VMEM((SUBLANES + tt, d), F32), pltpu.VMEM((1, d), F32)],
        compiler_params=_params("parallel", "arbitrary"),
        name="rglru_mixer",
    )(h, gain.reshape(1, dm), w_in.astype(MXU_DTYPE), conv_w, vecs, w_a.astype(MXU_DTYPE),
      w_x.astype(MXU_DTYPE))


def kernel(x, p, norm_gains, mlp_w_up, mlp_w_down, ple_w_up, ple_w_gate, a_w_in, a_ig_bias, a_fg_bias, a_head_gain, a_w_out, b_w_in, b_lower_bound, b_head_gain, b_w_out, c_w_in, c_ln_gain, c_ln_bias, c_spatial_w, c_spatial_b, c_w_out, d_w_in, d_conv_w, d_conv_b, d_w_a, d_b_a, d_w_x, d_b_x, d_lambda, d_w_out):
    bsz, seq, d = x.shape
    depth = norm_gains.shape[0]
    n_mixers = 4
    h = x.reshape(bsz * seq, d)
    for i in range(depth):
        kind, j = i % n_mixers, i // n_mixers
        g = norm_gains[i]
        if kind == 0:
            y = _mlstm_mixer(h, g[0], a_w_in[j], a_ig_bias[j], a_fg_bias[j], a_head_gain[j], bsz, seq)
            w_out = a_w_out[j]
        elif kind == 1:
            y = _hgrn2_mixer(h, g[0], b_w_in[j], b_lower_bound, b_head_gain[j], i, bsz, seq)
            w_out = b_w_out[j]
        elif kind == 2:
            y = _gmlp_mixer(h, g[0], c_w_in[j], c_ln_gain[j], c_ln_bias[j], c_spatial_w[j], c_spatial_b[j])
            w_out = c_w_out[j]
        else:
            y = _rglru_mixer(h, g[0], d_w_in[j], d_conv_w[j], d_conv_b[j], d_w_a[j], d_b_a[j], d_w_x[j],
                             d_b_x[j], d_lambda[j], bsz, seq)
            w_out = d_w_out[j]
        h = _post_mixer(h, y, p[i].reshape(bsz * seq, -1), g[1:5], w_out, mlp_w_up[i], mlp_w_down[i],
                        ple_w_gate[i], ple_w_up[i])
    return h.reshape(bsz, seq, d)
```

```python
import functools

import jax
import jax.numpy as jnp
from jax import lax
from jax.experimental import pallas as pl
from jax.experimental.pallas import tpu as pltpu

F32 = jnp.float32
MXU_DTYPE = jnp.bfloat16
NORM_EPS = 1e-6
RG_C = 8.0

LANES = 128
SUBLANES = 8
SQRT_FLOOR = 1e-30
CHUNK = 128
VMEM_LIMIT = 63 * 1024 * 1024

A_HEADS, A_DQK, A_DV = 8, 64, 128
B_HEADS, B_DK = 8, 128
C_GROUPS = 8
D_BLOCKS, D_CONV = 4, 4


def _params(*sem):
    return pltpu.CompilerParams(dimension_semantics=sem, vmem_limit_bytes=VMEM_LIMIT)


def _mm(a, b):
    return jnp.dot(a.astype(MXU_DTYPE), b.astype(MXU_DTYPE), preferred_element_type=F32)


def _mm_nt(a, b):
    return lax.dot_general(a.astype(MXU_DTYPE), b.astype(MXU_DTYPE),
                           (((1,), (1,)), ((), ())), preferred_element_type=F32)


def _rms(x, gain):
    return x * lax.rsqrt(jnp.mean(x * x, axis=-1, keepdims=True) + NORM_EPS) * gain


def _sigmoid(x):
    return 0.5 * jnp.tanh(0.5 * x) + 0.5


def _log_sigmoid(x):
    return jnp.minimum(x, 0.0) - jnp.log(1.0 + jnp.exp(-jnp.abs(x)))


_GELU_K = 0.7978845608028654


def _gelu_tanh(x):
    hx = 0.5 * x
    return hx + hx * jnp.tanh(x * (_GELU_K + (_GELU_K * 0.044715) * (x * x)))


def _hgrn2_gates(fz, lbp, layer):
    e = jnp.exp(lbp - jnp.max(lbp, axis=0, keepdims=True))
    sm = e / jnp.sum(e, axis=0, keepdims=True)
    cs = sm[0:1, :]
    first = cs
    for r in range(1, layer + 1):
        cs = cs + sm[r:r + 1, :]
    lb = cs - first
    t = jnp.exp(-jnp.abs(fz))
    r = 1.0 / (1.0 + t)
    sig_abs, sig_nabs = r, t * r
    pos = fz >= 0.0
    sig = jnp.where(pos, sig_abs, sig_nabs)
    sig_neg = jnp.where(pos, sig_nabs, sig_abs)
    return jnp.log(lb + (1.0 - lb) * sig), (1.0 - lb) * sig_neg


def _const_spec(shape):
    nd = len(shape)
    return pl.BlockSpec(shape, lambda *_: (0,) * nd)


def _single_buffered(shape):
    nd = len(shape)
    return pl.BlockSpec(shape, lambda *_: (0,) * nd, pipeline_mode=pl.Buffered(1))


def _post_mixer_kernel(h_ref, y_ref, p_ref, g_ref, wo_ref, wu_ref, wd_ref, wg_ref, wp_ref, o_ref,
                       *, ff_chunk, sub_rows):
    n_ff = wu_ref.shape[1] // ff_chunk
    n_sub = h_ref.shape[0] // sub_rows
    rows = [slice(r * sub_rows, (r + 1) * sub_rows) for r in range(n_sub)]

    def head(r):
        h = h_ref[rows[r], :] + _rms(jnp.dot(y_ref[rows[r], :], wo_ref[...], preferred_element_type=F32),
                                     g_ref[0:1, :])
        return h, _rms(h, g_ref[1:2, :]).astype(MXU_DTYPE), jnp.zeros(h.shape, F32)

    def ff(hn, acc, c):
        u = jnp.dot(hn, wu_ref[:, c * ff_chunk:(c + 1) * ff_chunk], preferred_element_type=F32)
        u = jnp.maximum(u, 0.0)
        u = (u * u).astype(MXU_DTYPE)
        return acc + jnp.dot(u, wd_ref[c * ff_chunk:(c + 1) * ff_chunk, :], preferred_element_type=F32)

    def tail(r, h, acc):
        h = h + _rms(acc, g_ref[2:3, :])
        gate = _sigmoid(jnp.dot(h.astype(MXU_DTYPE), wg_ref[...], preferred_element_type=F32))
        emb = jnp.dot(p_ref[rows[r], :].astype(MXU_DTYPE), wp_ref[...], preferred_element_type=F32)
        o_ref[rows[r], :] = h + _rms(gate * emb, g_ref[3:4, :])

    half = n_ff // 2
    prev = None
    for r in range(n_sub):
        h, hn, acc = head(r)
        if prev is not None:
            ph, phn, pacc = prev
            for c in range(half, n_ff):
                pacc = ff(phn, pacc, c)
        for c in range(half):
            acc = ff(hn, acc, c)
        if prev is not None:
            tail(r - 1, ph, pacc)
        prev = (h, hn, acc)
    ph, phn, pacc = prev
    for c in range(half, n_ff):
        pacc = ff(phn, pacc, c)
    tail(n_sub - 1, ph, pacc)


def _post_mixer(h, y, p, gains4, w_out, w_up, w_down, w_gate, w_ple, tm=1024, sub_rows=256, ff_chunk=1024):
    m, d = h.shape
    k = y.shape[1]
    dp = p.shape[1]
    weights = [w.astype(MXU_DTYPE) for w in (w_out, w_up, w_down, w_gate, w_ple)]
    return pl.pallas_call(
        functools.partial(_post_mixer_kernel, ff_chunk=ff_chunk, sub_rows=sub_rows),
        grid=(m // tm,),
        in_specs=[pl.BlockSpec((tm, d), lambda i: (i, 0)),
                  pl.BlockSpec((tm, k), lambda i: (i, 0)),
                  pl.BlockSpec((tm, dp), lambda i: (i, 0)),
                  _const_spec((4, d))] + [_single_buffered(w.shape) for w in weights],
        out_specs=pl.BlockSpec((tm, d), lambda i: (i, 0)),
        out_shape=jax.ShapeDtypeStruct((m, d), F32),
        compiler_params=_params("parallel"),
        name="post_mixer",
    )(h, y, p, gains4, *weights)


def _tril_mask(n):
    r = lax.broadcasted_iota(jnp.int32, (n, n), 0)
    c = lax.broadcasted_iota(jnp.int32, (n, n), 1)
    return r >= c


def _mlstm_gates(pre, m_ref):
    L = pre.shape[0]
    lane = lax.broadcasted_iota(jnp.int32, (L, LANES), 1)
    t_idx = lax.broadcasted_iota(jnp.int32, (L, LANES), 0)
    val = jnp.where(lane < A_HEADS, pre, jnp.where(lane < 2 * A_HEADS, _log_sigmoid(pre), 0.0))
    bc = jnp.dot(_tril_mask(L).astype(F32), val, preferred_element_type=F32,
                 precision=lax.Precision.HIGHEST)
    b = pltpu.roll(bc, LANES - A_HEADS, 1)
    a = val - b
    cm = a
    step = 1
    while step < L:
        cm = jnp.maximum(cm, jnp.where(t_idx >= step, pltpu.roll(cm, step, 0), -jnp.inf))
        step *= 2
    m_st = m_ref[...]
    mx = jnp.maximum(cm, m_st)
    mx_last = mx[L - 1:L, :]
    m_ref[...] = b[L - 1:L, :] + mx_last
    return dict(
        mx=mx,
        a_t=a.T,
        wk_t=jnp.exp(a - mx_last).T,
        decay=jnp.exp(m_st - mx_last),
        w_inter=jnp.exp(m_st - mx) * A_DQK ** -0.5,
        e_negm=jnp.exp(-(b + mx)))


def _mlstm_heads(gt, q, k, v, o_gate, hg_ref, c_ref, y_ref, rows):
    L = q.shape[0]
    causal = _tril_mask(L)
    scale = A_DQK ** -0.5
    k_t = k.T
    ones_col = (lax.broadcasted_iota(jnp.int32, (L, A_DV), 1) == 0).astype(MXU_DTYPE)
    heads = range(A_HEADS)
    qs = [q[:, hd * A_DQK:(hd + 1) * A_DQK].astype(MXU_DTYPE) for hd in heads]
    kts = [k_t[hd * A_DQK:(hd + 1) * A_DQK, :] for hd in heads]
    vs = [v[:, hd * A_DV:(hd + 1) * A_DV].astype(MXU_DTYPE) for hd in heads]
    cs = [c_ref[hd] for hd in heads]
    dmats = [jnp.exp(jnp.where(causal, gt["a_t"][hd:hd + 1, :] - gt["mx"][:, hd:hd + 1], -jnp.inf))
             for hd in heads]
    ss = [_mm(qs[hd], kts[hd]) * scale * dmats[hd] for hd in heads]
    inters = [_mm(qs[hd], cs[hd]) for hd in heads]
    for hd in heads:
        v_ext = jnp.concatenate([vs[hd], ones_col], axis=1)
        c_ref[hd] = (gt["decay"][:, hd:hd + 1] * cs[hd]
                     + _mm(kts[hd] * gt["wk_t"][hd:hd + 1, :], v_ext))
    wis = [gt["w_inter"][:, hd:hd + 1] for hd in heads]
    nums = [_mm(ss[hd], vs[hd]) + wis[hd] * inters[hd][:, :A_DV] for hd in heads]
    dens = [jnp.sum(ss[hd], axis=1, keepdims=True) + wis[hd] * inters[hd][:, A_DV:A_DV + 1] for hd in heads]
    outs = [nums[hd] * (1.0 / jnp.maximum(jnp.abs(dens[hd]), gt["e_negm"][:, hd:hd + 1])) for hd in heads]
    for hd in heads:
        sl = slice(hd * A_DV, (hd + 1) * A_DV)
        y_ref[rows, sl] = (o_gate[:, sl] * _rms(outs[hd], hg_ref[:, sl])).astype(y_ref.dtype)


def _mlstm_kernel(h_ref, g_ref, w_ref, bias_ref, hg_ref, y_ref, c_ref, m_ref):
    tm, d = y_ref.shape
    qk_w = A_HEADS * A_DQK

    @pl.when(pl.program_id(1) == 0)
    def _():
        c_ref[...] = jnp.zeros(c_ref.shape, F32)
        m_ref[...] = jnp.zeros(m_ref.shape, F32)

    hn = _rms(h_ref[...], g_ref[...]).astype(MXU_DTYPE)
    proj = lambda lo, hi: jnp.dot(hn, w_ref[:, lo:hi], preferred_element_type=F32)
    pre = proj(2 * qk_w + 2 * d, 2 * qk_w + 2 * d + LANES) + bias_ref[...]
    q = proj(0, qk_w)
    k = proj(qk_w, 2 * qk_w)
    chunks = [slice(c * CHUNK, (c + 1) * CHUNK) for c in range(tm // CHUNK)]
    gt = _mlstm_gates(pre[chunks[0], :], m_ref)
    v = proj(2 * qk_w, 2 * qk_w + d)
    o_gate = _sigmoid(proj(2 * qk_w + d, 2 * qk_w + 2 * d))
    for c, rows in enumerate(chunks):
        if c > 0:
            gt = _mlstm_gates(pre[rows, :], m_ref)
        _mlstm_heads(gt, q[rows, :], k[rows, :], v[rows, :], o_gate[rows, :], hg_ref, c_ref, y_ref, rows)


def _mlstm_mixer(h, gain, w_in, ig_bias, fg_bias, head_gain, bsz, seq, tm=2 * CHUNK):
    m, dm = h.shape
    d = A_HEADS * A_DV
    nt = seq // tm
    pad = LANES - 2 * A_HEADS
    w_in = jnp.pad(w_in, ((0, 0), (0, pad)))
    bias = jnp.pad(jnp.concatenate([ig_bias, fg_bias]), (0, pad)).reshape(1, LANES)
    return pl.pallas_call(
        _mlstm_kernel,
        grid=(bsz, nt),
        in_specs=[pl.BlockSpec((tm, dm), lambda b, t: (b * nt + t, 0)),
                  _const_spec((1, dm)),
                  _single_buffered(w_in.shape),
                  _const_spec((1, LANES)),
                  _const_spec((1, d))],
        out_specs=pl.BlockSpec((tm, d), lambda b, t: (b * nt + t, 0)),
        out_shape=jax.ShapeDtypeStruct((m, d), MXU_DTYPE),
        scratch_shapes=[pltpu.VMEM((A_HEADS, A_DQK, 2 * A_DV), F32),
                        pltpu.VMEM((1, LANES), F32)],
        compiler_params=_params("parallel", "arbitrary"),
        name="mlstm_mixer",
    )(h, gain.reshape(1, dm), w_in.astype(MXU_DTYPE), bias, head_gain.reshape(1, d))


def _block_ref_rows(g, w):
    L, D = g.shape
    if 2 * w >= SUBLANES:
        g3 = g.reshape(L // (2 * w), 2 * w, D)
        return jnp.broadcast_to(g3[:, w - 1:w, :], g3.shape).reshape(L, D)
    g3 = g.reshape(L // SUBLANES, SUBLANES, D)
    sub = lax.broadcasted_iota(jnp.int32, g3.shape, 1)
    out = None
    for start in range(0, SUBLANES, 2 * w):
        r = jnp.broadcast_to(g3[:, start + w - 1:start + w, :], g3.shape)
        out = r if out is None else jnp.where(sub >= start, r, out)
    return out.reshape(L, D)


def _split3(x):
    hi = x.astype(MXU_DTYPE)
    r1 = x - hi.astype(F32)
    mid = r1.astype(MXU_DTYPE)
    lo = (r1 - mid.astype(F32)).astype(MXU_DTYPE)
    return jnp.concatenate([hi, mid, lo], axis=1)


def _dot3(mat, x3, n):
    r = jnp.dot(mat, x3, preferred_element_type=F32)
    return r[:, 0:n] + r[:, n:2 * n] + r[:, 2 * n:3 * n]


def _hgrn2_attn(q, log_f, k):
    L, D = q.shape
    dk = B_DK
    row = lax.broadcasted_iota(jnp.int32, (L, L), 0)
    col = lax.broadcasted_iota(jnp.int32, (L, L), 1)
    trow = lax.broadcasted_iota(jnp.int32, (L, 1), 0)
    diag = row == col
    g = _dot3((row >= col).astype(MXU_DTYPE), _split3(log_f), D)

    attn = []
    for hd in range(B_HEADS):
        sl = slice(hd * dk, (hd + 1) * dk)
        attn.append(jnp.where(diag, jnp.sum(q[:, sl] * k[:, sl], axis=1, keepdims=True), 0.0))
    w = L // 2
    while w >= 1:
        upper = (trow % (2 * w)) >= w
        ref = _block_ref_rows(g, w)
        x = jnp.exp(jnp.where(upper, g - ref, ref - g))
        z = (jnp.where(upper, q, k) * x).astype(MXU_DTYPE)
        mask = ((row // (2 * w)) == (col // (2 * w))) & ((row % (2 * w)) >= w) & ((col % (2 * w)) < w)
        for hd in range(B_HEADS):
            zh = z[:, hd * dk:(hd + 1) * dk]
            attn[hd] = jnp.where(mask, _mm_nt(zh, zh), attn[hd])
        w //= 2
    return attn, g


def _hgrn2_tail(attn, g, q, k, iv, silu_g, hg_ref, st_ref, y_ref, rows):
    L = q.shape[0]
    dk = B_DK
    g_last = g[L - 1:L, :]
    q_dec = (q * jnp.exp(g)).astype(MXU_DTYPE)
    k_dec = (k * jnp.exp(g_last - g)).astype(MXU_DTYPE)
    s_dec = jnp.exp(g_last)
    heads = range(B_HEADS)
    sls = [slice(hd * dk, (hd + 1) * dk) for hd in heads]
    vs = [iv[:, sl] for sl in sls]
    sts = [st_ref[hd] for hd in heads]
    outs = [_mm(attn[hd], vs[hd]) + _mm_nt(q_dec[:, sls[hd]], sts[hd]) for hd in heads]
    for hd in heads:
        st_ref[hd] = s_dec[:, sls[hd]] * sts[hd] + _mm(vs[hd].T, k_dec[:, sls[hd]])
    for hd in heads:
        y = _rms(outs[hd], hg_ref[:, sls[hd]]) * silu_g[:, sls[hd]]
        y_ref[rows, sls[hd]] = y.astype(y_ref.dtype)


def _hgrn2_kernel(h_ref, g_ref, w_ref, lbp_ref, hg_ref, y_ref, st_ref, *, layer):
    tm, d = y_ref.shape

    @pl.when(pl.program_id(1) == 0)
    def _():
        st_ref[...] = jnp.zeros(st_ref.shape, F32)

    hn = _rms(h_ref[...], g_ref[...]).astype(MXU_DTYPE)
    proj = lambda j: jnp.dot(hn, w_ref[:, j * d:(j + 1) * d], preferred_element_type=F32)
    q = proj(0)
    log_f, k = _hgrn2_gates(proj(1), lbp_ref[...], layer)
    chunks = [slice(c * CHUNK, (c + 1) * CHUNK) for c in range(tm // CHUNK)]
    attn, g = _hgrn2_attn(q[chunks[0], :], log_f[chunks[0], :], k[chunks[0], :])
    iv = proj(2)
    gz = proj(3)
    silu_g = gz * _sigmoid(gz)
    for c, rows in enumerate(chunks):
        if c > 0:
            attn, g = _hgrn2_attn(q[rows, :], log_f[rows, :], k[rows, :])
        _hgrn2_tail(attn, g, q[rows, :], k[rows, :], iv[rows, :], silu_g[rows, :], hg_ref, st_ref, y_ref, rows)


def _hgrn2_mixer(h, gain, w_in, lower_bound, head_gain, layer, bsz, seq, tm=4 * CHUNK):
    m, dm = h.shape
    d = B_HEADS * B_DK
    nt = seq // tm
    return pl.pallas_call(
        functools.partial(_hgrn2_kernel, layer=layer),
        grid=(bsz, nt),
        in_specs=[pl.BlockSpec((tm, dm), lambda b, t: (b * nt + t, 0)),
                  _const_spec((1, dm)),
                  _single_buffered(w_in.shape),
                  _const_spec(lower_bound.shape),
                  _const_spec((1, d))],
        out_specs=pl.BlockSpec((tm, d), lambda b, t: (b * nt + t, 0)),
        out_shape=jax.ShapeDtypeStruct((m, d), MXU_DTYPE),
        scratch_shapes=[pltpu.VMEM((B_HEADS, B_DK, B_DK), F32)],
        compiler_params=_params("parallel", "arbitrary"),
        name="hgrn2_mixer",
    )(h, gain.reshape(1, dm), w_in.astype(MXU_DTYPE), lower_bound, head_gain.reshape(1, d))


def _gmlp_kernel(h_ref, g_ref, w_ref, lng_ref, lnb_ref, ws_ref, bs_ref, y_ref):
    tm = h_ref.shape[0]
    half = y_ref.shape[1]
    gw = half // C_GROUPS
    hn = _rms(h_ref[...], g_ref[...]).astype(MXU_DTYPE)
    v = _gelu_tanh(jnp.dot(hn, w_ref[:, half:], preferred_element_type=F32))
    u_group = lambda g: _gelu_tanh(jnp.dot(hn, w_ref[:, g * gw:(g + 1) * gw], preferred_element_type=F32))
    ahead = 2
    us = [u_group(g) for g in range(ahead)]
    mu = jnp.mean(v, axis=-1, keepdims=True)
    vc = v - mu
    var = jnp.mean(vc * vc, axis=-1, keepdims=True)
    vn = (vc * lax.rsqrt(var + NORM_EPS) * lng_ref[...] + lnb_ref[...]).astype(MXU_DTYPE)
    causal = _tril_mask(CHUNK)
    for g in range(C_GROUPS):
        sl = slice(g * gw, (g + 1) * gw)
        if g + ahead < C_GROUPS:
            us.append(u_group(g + ahead))
        u = us[g]
        w = jnp.where(causal, ws_ref[g], 0.0).astype(MXU_DTYPE)
        for c in range(tm // CHUNK):
            rows = slice(c * CHUNK, (c + 1) * CHUNK)
            vm = jnp.dot(w, vn[rows, sl], preferred_element_type=F32) + bs_ref[:, g:g + 1]
            y_ref[rows, sl] = (u[rows, :] * vm).astype(y_ref.dtype)


def _gmlp_mixer(h, gain, w_in, ln_gain, ln_bias, w_s, b_s, tm=1024):
    m, d = h.shape
    half = w_in.shape[1] // 2
    return pl.pallas_call(
        _gmlp_kernel,
        grid=(m // tm,),
        in_specs=[pl.BlockSpec((tm, d), lambda i: (i, 0)),
                  _const_spec((1, d)),
                  _single_buffered(w_in.shape),
                  _const_spec((1, half)),
                  _const_spec((1, half)),
                  _const_spec(w_s.shape),
                  _const_spec((CHUNK, C_GROUPS))],
        out_specs=pl.BlockSpec((tm, half), lambda i: (i, 0)),
        out_shape=jax.ShapeDtypeStruct((m, half), MXU_DTYPE),
        compiler_params=_params("parallel"),
        name="gmlp_mixer",
    )(h, gain.reshape(1, d), w_in.astype(MXU_DTYPE), ln_gain.reshape(1, half), ln_bias.reshape(1, half),
      w_s, b_s.T)


def _rglru_kernel(h_ref, g_ref, w_ref, cw_ref, vec_ref, wa_ref, wx_ref, y_ref, xpad_ref, hc_ref):
    T, d = y_ref.shape
    halo = xpad_ref.shape[0] - T

    @pl.when(pl.program_id(1) == 0)
    def _():
        xpad_ref[0:halo, :] = jnp.zeros((halo, d), F32)
        hc_ref[...] = jnp.zeros(hc_ref.shape, F32)

    hn = _rms(h_ref[...], g_ref[...]).astype(MXU_DTYPE)
    xb = jnp.dot(hn, w_ref[:, d:], preferred_element_type=F32)
    xpad_ref[halo:, :] = xb
    xc = xb * cw_ref[D_CONV - 1:D_CONV, :] + vec_ref[0:1, :]
    for j in range(1, D_CONV):
        shifted = xpad_ref[halo - j:halo - j + T, :]
        xc = xc + shifted * cw_ref[D_CONV - 1 - j:D_CONV - j, :]
    xpad_ref[0:halo, :] = xb[T - halo:T, :]

    bw = d // D_BLOCKS
    r_parts, i_parts = [], []
    for n in range(D_BLOCKS):
        xn = xc[:, n * bw:(n + 1) * bw]
        r_parts.append(_mm(xn, wa_ref[n]))
        i_parts.append(_mm(xn, wx_ref[n]))
    gate = _gelu_tanh(jnp.dot(hn, w_ref[:, :d], preferred_element_type=F32))
    r = _sigmoid(jnp.concatenate(r_parts, axis=1) + vec_ref[1:2, :])
    ig = _sigmoid(jnp.concatenate(i_parts, axis=1) + vec_ref[2:3, :])
    nlam = -vec_ref[3:4, :]
    softplus = jnp.maximum(nlam, 0.0) + jnp.log1p(jnp.exp(-jnp.abs(nlam)))
    a = jnp.exp(-RG_C * r * softplus)
    om = 1.0 - a * a
    b = (om * lax.rsqrt(jnp.maximum(om, SQRT_FLOOR))) * (ig * xc)

    groups = T // SUBLANES
    a = a.reshape(groups, SUBLANES, d)
    b = b.reshape(groups, SUBLANES, d)
    sub = lax.broadcasted_iota(jnp.int32, a.shape, 1)
    step = 1
    while step < SUBLANES:
        keep = sub >= step
        a_prev = jnp.where(keep, pltpu.roll(a, step, 1), 1.0)
        b_prev = jnp.where(keep, pltpu.roll(b, step, 1), 0.0)
        b = a * b_prev + b
        a = a * a_prev
        step *= 2
    carry = hc_ref[...]
    rows = []
    for g in range(groups):
        hg = b[g] + a[g] * carry
        carry = hg[SUBLANES - 1:SUBLANES, :]
        rows.append(hg)
    hc_ref[...] = carry
    y_ref[...] = (jnp.concatenate(rows, axis=0) * gate).astype(y_ref.dtype)


def _rglru_mixer(h, gain, w_in, conv_w, conv_b, w_a, b_a, w_x, b_x, lam, bsz, seq, tt=512):
    m, dm = h.shape
    d = w_in.shape[1] // 2
    nt = seq // tt
    row = lambda b, t: b * nt + t
    vecs = jnp.stack([conv_b, b_a, b_x, lam])
    return pl.pallas_call(
        _rglru_kernel,
        grid=(bsz, nt),
        in_specs=[pl.BlockSpec((tt, dm), lambda b, t: (row(b, t), 0)),
                  _const_spec((1, dm)),
                  _single_buffered(w_in.shape),
                  _const_spec(conv_w.shape),
                  _const_spec(vecs.shape),
                  _const_spec(w_a.shape),
                  _const_spec(w_x.shape)],
        out_specs=pl.BlockSpec((tt, d), lambda b, t: (row(b, t), 0)),
        out_shape=jax.ShapeDtypeStruct((m, d), MXU_DTYPE),
        scratch_shapes=[pltpu.VMEM((SUBLANES + tt, d), F32), pltpu.VMEM((1, d), F32)],
        compiler_params=_params("parallel", "arbitrary"),
        name="rglru_mixer",
    )(h, gain.reshape(1, dm), w_in.astype(MXU_DTYPE), conv_w, vecs, w_a.astype(MXU_DTYPE),
      w_x.astype(MXU_DTYPE))


def kernel(x, p, norm_gains, mlp_w_up, mlp_w_down, ple_w_up, ple_w_gate, a_w_in, a_ig_bias, a_fg_bias, a_head_gain, a_w_out, b_w_in, b_lower_bound, b_head_gain, b_w_out, c_w_in, c_ln_gain, c_ln_bias, c_spatial_w, c_spatial_b, c_w_out, d_w_in, d_conv_w, d_conv_b, d_w_a, d_b_a, d_w_x, d_b_x, d_lambda, d_w_out):
    bsz, seq, d = x.shape
    depth = norm_gains.shape[0]
    n_mixers = 4
    h = x.reshape(bsz * seq, d)
    for i in range(depth):
        kind, j = i % n_mixers, i // n_mixers
        g = norm_gains[i]
        if kind == 0:
            y = _mlstm_mixer(h, g[0], a_w_in[j], a_ig_bias[j], a_fg_bias[j], a_head_gain[j], bsz, seq)
            w_out = a_w_out[j]
        elif kind == 1:
            y = _hgrn2_mixer(h, g[0], b_w_in[j], b_lower_bound, b_head_gain[j], i, bsz, seq)
            w_out = b_w_out[j]
        elif kind == 2:
            y = _gmlp_mixer(h, g[0], c_w_in[j], c_ln_gain[j], c_ln_bias[j], c_spatial_w[j], c_spatial_b[j])
            w_out = c_w_out[j]
        else:
            y = _rglru_mixer(h, g[0], d_w_in[j], d_conv_w[j], d_conv_b[j], d_w_a[j], d_b_a[j], d_w_x[j],
                             d_b_x[j], d_lambda[j], bsz, seq)
            w_out = d_w_out[j]
        h = _post_mixer(h, y, p[i].reshape(bsz * seq, -1), g[1:5], w_out, mlp_w_up[i], mlp_w_down[i],
                        ple_w_gate[i], ple_w_up[i])
    return h.reshape(bsz, seq, d)
```

```python
import functools

import jax
import jax.numpy as jnp
from jax import lax
from jax.experimental import pallas as pl
from jax.experimental.pallas import tpu as pltpu

F32 = jnp.float32
MXU_DTYPE = jnp.bfloat16
NORM_EPS = 1e-6
RG_C = 8.0

LANES = 128
SUBLANES = 8
SQRT_FLOOR = 1e-30
LOG2_E = 1.4426950408889634
CHUNK = 128
VMEM_LIMIT = 63 * 1024 * 1024

A_HEADS, A_DQK, A_DV = 8, 64, 128
B_HEADS, B_DK = 8, 128
C_GROUPS = 8
D_BLOCKS, D_CONV = 4, 4


def _params(*sem):
    return pltpu.CompilerParams(dimension_semantics=sem, vmem_limit_bytes=VMEM_LIMIT)


def _mm(a, b):
    return jnp.dot(a.astype(MXU_DTYPE), b.astype(MXU_DTYPE), preferred_element_type=F32)


def _mm_nt(a, b):
    return lax.dot_general(a.astype(MXU_DTYPE), b.astype(MXU_DTYPE),
                           (((1,), (1,)), ((), ())), preferred_element_type=F32)


def _rms(x, gain):
    return x * lax.rsqrt(jnp.mean(x * x, axis=-1, keepdims=True) + NORM_EPS) * gain


def _sigmoid(x):
    return 0.5 * jnp.tanh(0.5 * x) + 0.5


def _log_sigmoid(x):
    return jnp.minimum(x, 0.0) - jnp.log(1.0 + jnp.exp(-jnp.abs(x)))


_GELU_K = 0.7978845608028654


def _gelu_tanh(x):
    hx = 0.5 * x
    return hx + hx * jnp.tanh(x * (_GELU_K + (_GELU_K * 0.044715) * (x * x)))


def _hgrn2_gates(fz, lbp, layer):
    e = jnp.exp(lbp - jnp.max(lbp, axis=0, keepdims=True))
    sm = e / jnp.sum(e, axis=0, keepdims=True)
    cs = sm[0:1, :]
    first = cs
    for r in range(1, layer + 1):
        cs = cs + sm[r:r + 1, :]
    lb = cs - first
    t = jnp.exp(-jnp.abs(fz))
    r = 1.0 / (1.0 + t)
    sig_abs, sig_nabs = r, t * r
    pos = fz >= 0.0
    sig = jnp.where(pos, sig_abs, sig_nabs)
    sig_neg = jnp.where(pos, sig_nabs, sig_abs)
    return jnp.log2(lb + (1.0 - lb) * sig), (1.0 - lb) * sig_neg


def _const_spec(shape):
    nd = len(shape)
    return pl.BlockSpec(shape, lambda *_: (0,) * nd)


def _single_buffered(shape):
    nd = len(shape)
    return pl.BlockSpec(shape, lambda *_: (0,) * nd, pipeline_mode=pl.Buffered(1))


def _post_mixer_kernel(h_ref, y_ref, p_ref, g_ref, wo_ref, wu_ref, wd_ref, wg_ref, wp_ref, o_ref,
                       *, ff_chunk, sub_rows):
    n_ff = wu_ref.shape[1] // ff_chunk
    n_sub = h_ref.shape[0] // sub_rows
    rows = [slice(r * sub_rows, (r + 1) * sub_rows) for r in range(n_sub)]

    def head(r):
        h = h_ref[rows[r], :] + _rms(jnp.dot(y_ref[rows[r], :], wo_ref[...], preferred_element_type=F32),
                                     g_ref[0:1, :])
        return h, _rms(h, g_ref[1:2, :]).astype(MXU_DTYPE), jnp.zeros(h.shape, F32)

    def ff(hn, acc, c):
        u = jnp.dot(hn, wu_ref[:, c * ff_chunk:(c + 1) * ff_chunk], preferred_element_type=F32)
        u = jnp.maximum(u, 0.0)
        u = (u * u).astype(MXU_DTYPE)
        return acc + jnp.dot(u, wd_ref[c * ff_chunk:(c + 1) * ff_chunk, :], preferred_element_type=F32)

    def tail(r, h, acc):
        h = h + _rms(acc, g_ref[2:3, :])
        gate = _sigmoid(jnp.dot(h.astype(MXU_DTYPE), wg_ref[...], preferred_element_type=F32))
        emb = jnp.dot(p_ref[rows[r], :].astype(MXU_DTYPE), wp_ref[...], preferred_element_type=F32)
        o_ref[rows[r], :] = h + _rms(gate * emb, g_ref[3:4, :])

    half = n_ff // 2
    prev = None
    for r in range(n_sub):
        h, hn, acc = head(r)
        if prev is not None:
            ph, phn, pacc = prev
            for c in range(half, n_ff):
                pacc = ff(phn, pacc, c)
        for c in range(half):
            acc = ff(hn, acc, c)
        if prev is not None:
            tail(r - 1, ph, pacc)
        prev = (h, hn, acc)
    ph, phn, pacc = prev
    for c in range(half, n_ff):
        pacc = ff(phn, pacc, c)
    tail(n_sub - 1, ph, pacc)


def _post_mixer(h, y, p, gains4, w_out, w_up, w_down, w_gate, w_ple, tm=1024, sub_rows=256, ff_chunk=1024):
    m, d = h.shape
    k = y.shape[1]
    dp = p.shape[1]
    weights = [w.astype(MXU_DTYPE) for w in (w_out, w_up, w_down, w_gate, w_ple)]
    return pl.pallas_call(
        functools.partial(_post_mixer_kernel, ff_chunk=ff_chunk, sub_rows=sub_rows),
        grid=(m // tm,),
        in_specs=[pl.BlockSpec((tm, d), lambda i: (i, 0)),
                  pl.BlockSpec((tm, k), lambda i: (i, 0)),
                  pl.BlockSpec((tm, dp), lambda i: (i, 0)),
                  _const_spec((4, d))] + [_single_buffered(w.shape) for w in weights],
        out_specs=pl.BlockSpec((tm, d), lambda i: (i, 0)),
        out_shape=jax.ShapeDtypeStruct((m, d), F32),
        compiler_params=_params("parallel"),
        name="post_mixer",
    )(h, y, p, gains4, *weights)


def _tril_mask(n):
    r = lax.broadcasted_iota(jnp.int32, (n, n), 0)
    c = lax.broadcasted_iota(jnp.int32, (n, n), 1)
    return r >= c


def _mlstm_gates(pre, m_ref):
    L = pre.shape[0]
    lane = lax.broadcasted_iota(jnp.int32, (L, LANES), 1)
    t_idx = lax.broadcasted_iota(jnp.int32, (L, LANES), 0)
    val = jnp.where(lane < A_HEADS, pre, jnp.where(lane < 2 * A_HEADS, _log_sigmoid(pre), 0.0))
    bc = jnp.dot(_tril_mask(L).astype(F32), val, preferred_element_type=F32,
                 precision=lax.Precision.HIGHEST)
    b = pltpu.roll(bc, LANES - A_HEADS, 1)
    a = val - b
    cm = a
    step = 1
    while step < L:
        cm = jnp.maximum(cm, jnp.where(t_idx >= step, pltpu.roll(cm, step, 0), -jnp.inf))
        step *= 2
    m_st = m_ref[...]
    mx = jnp.maximum(cm, m_st)
    mx_last = mx[L - 1:L, :]
    m_ref[...] = b[L - 1:L, :] + mx_last
    return dict(
        mx=mx,
        a_t=a.T,
        wk_t=jnp.exp(a - mx_last).T,
        decay=jnp.exp(m_st - mx_last),
        w_inter=jnp.exp(m_st - mx) * A_DQK ** -0.5,
        e_negm=jnp.exp(-(b + mx)))


def _mlstm_heads(gt, q, k, v, o_gate, hg_ref, c_ref, y_ref, rows):
    L = q.shape[0]
    causal = _tril_mask(L)
    scale = A_DQK ** -0.5
    k_t = k.T
    ones_col = (lax.broadcasted_iota(jnp.int32, (L, A_DV), 1) == 0).astype(MXU_DTYPE)
    heads = range(A_HEADS)
    qs = [q[:, hd * A_DQK:(hd + 1) * A_DQK].astype(MXU_DTYPE) for hd in heads]
    kts = [k_t[hd * A_DQK:(hd + 1) * A_DQK, :] for hd in heads]
    vs = [v[:, hd * A_DV:(hd + 1) * A_DV].astype(MXU_DTYPE) for hd in heads]
    cs = [c_ref[hd] for hd in heads]
    dmats = [jnp.exp(jnp.where(causal, gt["a_t"][hd:hd + 1, :] - gt["mx"][:, hd:hd + 1], -jnp.inf))
             for hd in heads]
    ss = [_mm(qs[hd], kts[hd]) * scale * dmats[hd] for hd in heads]
    inters = [_mm(qs[hd], cs[hd]) for hd in heads]
    for hd in heads:
        v_ext = jnp.concatenate([vs[hd], ones_col], axis=1)
        c_ref[hd] = (gt["decay"][:, hd:hd + 1] * cs[hd]
                     + _mm(kts[hd] * gt["wk_t"][hd:hd + 1, :], v_ext))
    wis = [gt["w_inter"][:, hd:hd + 1] for hd in heads]
    nums = [_mm(ss[hd], vs[hd]) + wis[hd] * inters[hd][:, :A_DV] for hd in heads]
    dens = [jnp.sum(ss[hd], axis=1, keepdims=True) + wis[hd] * inters[hd][:, A_DV:A_DV + 1] for hd in heads]
    outs = [nums[hd] * (1.0 / jnp.maximum(jnp.abs(dens[hd]), gt["e_negm"][:, hd:hd + 1])) for hd in heads]
    for hd in heads:
        sl = slice(hd * A_DV, (hd + 1) * A_DV)
        y_ref[rows, sl] = (o_gate[:, sl] * _rms(outs[hd], hg_ref[:, sl])).astype(y_ref.dtype)


def _mlstm_kernel(h_ref, g_ref, w_ref, bias_ref, hg_ref, y_ref, c_ref, m_ref):
    tm, d = y_ref.shape
    qk_w = A_HEADS * A_DQK

    @pl.when(pl.program_id(1) == 0)
    def _():
        c_ref[...] = jnp.zeros(c_ref.shape, F32)
        m_ref[...] = jnp.zeros(m_ref.shape, F32)

    hn = _rms(h_ref[...], g_ref[...]).astype(MXU_DTYPE)
    proj = lambda lo, hi: jnp.dot(hn, w_ref[:, lo:hi], preferred_element_type=F32)
    pre = proj(2 * qk_w + 2 * d, 2 * qk_w + 2 * d + LANES) + bias_ref[...]
    q = proj(0, qk_w)
    k = proj(qk_w, 2 * qk_w)
    chunks = [slice(c * CHUNK, (c + 1) * CHUNK) for c in range(tm // CHUNK)]
    gt = _mlstm_gates(pre[chunks[0], :], m_ref)
    v = proj(2 * qk_w, 2 * qk_w + d)
    o_gate = _sigmoid(proj(2 * qk_w + d, 2 * qk_w + 2 * d))
    for c, rows in enumerate(chunks):
        if c > 0:
            gt = _mlstm_gates(pre[rows, :], m_ref)
        _mlstm_heads(gt, q[rows, :], k[rows, :], v[rows, :], o_gate[rows, :], hg_ref, c_ref, y_ref, rows)


def _mlstm_mixer(h, gain, w_in, ig_bias, fg_bias, head_gain, bsz, seq, tm=2 * CHUNK):
    m, dm = h.shape
    d = A_HEADS * A_DV
    nt = seq // tm
    pad = LANES - 2 * A_HEADS
    w_in = jnp.pad(w_in, ((0, 0), (0, pad)))
    bias = jnp.pad(jnp.concatenate([ig_bias, fg_bias]), (0, pad)).reshape(1, LANES)
    return pl.pallas_call(
        _mlstm_kernel,
        grid=(bsz, nt),
        in_specs=[pl.BlockSpec((tm, dm), lambda b, t: (b * nt + t, 0)),
                  _const_spec((1, dm)),
                  _single_buffered(w_in.shape),
                  _const_spec((1, LANES)),
                  _const_spec((1, d))],
        out_specs=pl.BlockSpec((tm, d), lambda b, t: (b * nt + t, 0)),
        out_shape=jax.ShapeDtypeStruct((m, d), MXU_DTYPE),
        scratch_shapes=[pltpu.VMEM((A_HEADS, A_DQK, 2 * A_DV), F32),
                        pltpu.VMEM((1, LANES), F32)],
        compiler_params=_params("parallel", "arbitrary"),
        name="mlstm_mixer",
    )(h, gain.reshape(1, dm), w_in.astype(MXU_DTYPE), bias, head_gain.reshape(1, d))


def _block_ref_rows(g, w):
    L, D = g.shape
    if 2 * w >= SUBLANES:
        g3 = g.reshape(L // (2 * w), 2 * w, D)
        return jnp.broadcast_to(g3[:, w - 1:w, :], g3.shape).reshape(L, D)
    g3 = g.reshape(L // SUBLANES, SUBLANES, D)
    sub = lax.broadcasted_iota(jnp.int32, g3.shape, 1)
    out = None
    for start in range(0, SUBLANES, 2 * w):
        r = jnp.broadcast_to(g3[:, start + w - 1:start + w, :], g3.shape)
        out = r if out is None else jnp.where(sub >= start, r, out)
    return out.reshape(L, D)


def _split3(x):
    hi = x.astype(MXU_DTYPE)
    r1 = x - hi.astype(F32)
    mid = r1.astype(MXU_DTYPE)
    lo = (r1 - mid.astype(F32)).astype(MXU_DTYPE)
    return jnp.concatenate([hi, mid, lo], axis=1)


def _dot3(mat, x3, n):
    r = jnp.dot(mat, x3, preferred_element_type=F32)
    return r[:, 0:n] + r[:, n:2 * n] + r[:, 2 * n:3 * n]


def _hgrn2_attn(q, log2_f, k):
    L, D = q.shape
    dk = B_DK
    assert dk == L
    row = lax.broadcasted_iota(jnp.int32, (L, L), 0)
    col = lax.broadcasted_iota(jnp.int32, (L, L), 1)
    diag = row == col
    g = _dot3((row >= col).astype(MXU_DTYPE), _split3(log2_f), D)

    attn = []
    for hd in range(B_HEADS):
        sl = slice(hd * dk, (hd + 1) * dk)
        attn.append(jnp.where(diag, jnp.sum(q[:, sl] * k[:, sl], axis=1, keepdims=True), 0.0))
    w = L // 2
    while w >= 1:
        upper = (row % (2 * w)) >= w
        sign = jnp.where(upper, 1.0, -1.0)
        ref = _block_ref_rows(g, w)
        mask = ((row // (2 * w)) == (col // (2 * w))) & upper & ((col % (2 * w)) < w)
        for hd in range(B_HEADS):
            sl = slice(hd * dk, (hd + 1) * dk)
            x = jnp.exp2((g[:, sl] - ref[:, sl]) * sign)
            zh = (jnp.where(upper, q[:, sl], k[:, sl]) * x).astype(MXU_DTYPE)
            attn[hd] = jnp.where(mask, _mm_nt(zh, zh), attn[hd])
        w //= 2
    return attn, g


def _hgrn2_tail(attn, g, q, k, iv, silu_g, hg_ref, st_ref, y_ref, rows):
    L = q.shape[0]
    dk = B_DK
    g_last = g[L - 1:L, :]
    q_dec = (q * jnp.exp2(g)).astype(MXU_DTYPE)
    k_dec = (k * jnp.exp2(g_last - g)).astype(MXU_DTYPE)
    s_dec = jnp.exp2(g_last)
    heads = range(B_HEADS)
    sls = [slice(hd * dk, (hd + 1) * dk) for hd in heads]
    vs = [iv[:, sl] for sl in sls]
    sts = [st_ref[hd] for hd in heads]
    outs = [_mm(attn[hd], vs[hd]) + _mm_nt(q_dec[:, sls[hd]], sts[hd]) for hd in heads]
    for hd in heads:
        st_ref[hd] = s_dec[:, sls[hd]] * sts[hd] + _mm(vs[hd].T, k_dec[:, sls[hd]])
    for hd in heads:
        y = _rms(outs[hd], hg_ref[:, sls[hd]]) * silu_g[:, sls[hd]]
        y_ref[rows, sls[hd]] = y.astype(y_ref.dtype)


def _hgrn2_kernel(h_ref, g_ref, w_ref, lbp_ref, hg_ref, y_ref, st_ref, *, layer):
    tm, d = y_ref.shape

    @pl.when(pl.program_id(1) == 0)
    def _():
        st_ref[...] = jnp.zeros(st_ref.shape, F32)

    hn = _rms(h_ref[...], g_ref[...]).astype(MXU_DTYPE)
    proj = lambda j: jnp.dot(hn, w_ref[:, j * d:(j + 1) * d], preferred_element_type=F32)
    q = proj(0)
    log_f, k = _hgrn2_gates(proj(1), lbp_ref[...], layer)
    chunks = [slice(c * CHUNK, (c + 1) * CHUNK) for c in range(tm // CHUNK)]
    attn, g = _hgrn2_attn(q[chunks[0], :], log_f[chunks[0], :], k[chunks[0], :])
    iv = proj(2)
    gz = proj(3)
    silu_g = gz * _sigmoid(gz)
    for c, rows in enumerate(chunks):
        if c > 0:
            attn, g = _hgrn2_attn(q[rows, :], log_f[rows, :], k[rows, :])
        _hgrn2_tail(attn, g, q[rows, :], k[rows, :], iv[rows, :], silu_g[rows, :], hg_ref, st_ref, y_ref, rows)


def _hgrn2_mixer(h, gain, w_in, lower_bound, head_gain, layer, bsz, seq, tm=4 * CHUNK):
    m, dm = h.shape
    d = B_HEADS * B_DK
    nt = seq // tm
    return pl.pallas_call(
        functools.partial(_hgrn2_kernel, layer=layer),
        grid=(bsz, nt),
        in_specs=[pl.BlockSpec((tm, dm), lambda b, t: (b * nt + t, 0)),
                  _const_spec((1, dm)),
                  _single_buffered(w_in.shape),
                  _const_spec(lower_bound.shape),
                  _const_spec((1, d))],
        out_specs=pl.BlockSpec((tm, d), lambda b, t: (b * nt + t, 0)),
        out_shape=jax.ShapeDtypeStruct((m, d), MXU_DTYPE),
        scratch_shapes=[pltpu.VMEM((B_HEADS, B_DK, B_DK), F32)],
        compiler_params=_params("parallel", "arbitrary"),
        name="hgrn2_mixer",
    )(h, gain.reshape(1, dm), w_in.astype(MXU_DTYPE), lower_bound, head_gain.reshape(1, d))


def _gmlp_kernel(h_ref, g_ref, w_ref, lng_ref, lnb_ref, ws_ref, bs_ref, y_ref):
    tm = h_ref.shape[0]
    half = y_ref.shape[1]
    gw = half // C_GROUPS
    hn = _rms(h_ref[...], g_ref[...]).astype(MXU_DTYPE)
    v = _gelu_tanh(jnp.dot(hn, w_ref[:, half:], preferred_element_type=F32))
    u_group = lambda g: _gelu_tanh(jnp.dot(hn, w_ref[:, g * gw:(g + 1) * gw], preferred_element_type=F32))
    ahead = 2
    us = [u_group(g) for g in range(ahead)]
    mu = jnp.mean(v, axis=-1, keepdims=True)
    vc = v - mu
    var = jnp.mean(vc * vc, axis=-1, keepdims=True)
    vn = (vc * lax.rsqrt(var + NORM_EPS) * lng_ref[...] + lnb_ref[...]).astype(MXU_DTYPE)
    causal = _tril_mask(CHUNK)
    for g in range(C_GROUPS):
        sl = slice(g * gw, (g + 1) * gw)
        if g + ahead < C_GROUPS:
            us.append(u_group(g + ahead))
        u = us[g]
        w = jnp.where(causal, ws_ref[g], 0.0).astype(MXU_DTYPE)
        for c in range(tm // CHUNK):
            rows = slice(c * CHUNK, (c + 1) * CHUNK)
            vm = jnp.dot(w, vn[rows, sl], preferred_element_type=F32) + bs_ref[:, g:g + 1]
            y_ref[rows, sl] = (u[rows, :] * vm).astype(y_ref.dtype)


def _gmlp_mixer(h, gain, w_in, ln_gain, ln_bias, w_s, b_s, tm=1024):
    m, d = h.shape
    half = w_in.shape[1] // 2
    return pl.pallas_call(
        _gmlp_kernel,
        grid=(m // tm,),
        in_specs=[pl.BlockSpec((tm, d), lambda i: (i, 0)),
                  _const_spec((1, d)),
                  _single_buffered(w_in.shape),
                  _const_spec((1, half)),
                  _const_spec((1, half)),
                  _const_spec(w_s.shape),
                  _const_spec((CHUNK, C_GROUPS))],
        out_specs=pl.BlockSpec((tm, half), lambda i: (i, 0)),
        out_shape=jax.ShapeDtypeStruct((m, half), MXU_DTYPE),
        compiler_params=_params("parallel"),
        name="gmlp_mixer",
    )(h, gain.reshape(1, d), w_in.astype(MXU_DTYPE), ln_gain.reshape(1, half), ln_bias.reshape(1, half),
      w_s, b_s.T)


def _rglru_kernel(h_ref, g_ref, w_ref, cw_ref, vec_ref, wa_ref, wx_ref, y_ref, xpad_ref, hc_ref):
    T, d = y_ref.shape
    halo = xpad_ref.shape[0] - T

    @pl.when(pl.program_id(1) == 0)
    def _():
        xpad_ref[0:halo, :] = jnp.zeros((halo, d), F32)
        hc_ref[...] = jnp.zeros(hc_ref.shape, F32)

    hn = _rms(h_ref[...], g_ref[...]).astype(MXU_DTYPE)
    xb = jnp.dot(hn, w_ref[:, d:], preferred_element_type=F32)
    xpad_ref[halo:, :] = xb
    xc = xb * cw_ref[D_CONV - 1:D_CONV, :] + vec_ref[0:1, :]
    for j in range(1, D_CONV):
        shifted = xpad_ref[halo - j:halo - j + T, :]
        xc = xc + shifted * cw_ref[D_CONV - 1 - j:D_CONV - j, :]
    xpad_ref[0:halo, :] = xb[T - halo:T, :]

    bw = d // D_BLOCKS
    r_parts, i_parts = [], []
    for n in range(D_BLOCKS):
        xn = xc[:, n * bw:(n + 1) * bw]
        r_parts.append(_mm(xn, wa_ref[n]))
        i_parts.append(_mm(xn, wx_ref[n]))
    gate = _gelu_tanh(jnp.dot(hn, w_ref[:, :d], preferred_element_type=F32))
    ig = _sigmoid(jnp.concatenate(i_parts, axis=1) + vec_ref[2:3, :])
    nlam = -vec_ref[3:4, :]
    softplus = jnp.maximum(nlam, 0.0) + jnp.log1p(jnp.exp(-jnp.abs(nlam)))
    k2 = (-0.5 * RG_C * LOG2_E) * softplus
    a = jnp.exp2(k2 * jnp.tanh(0.5 * (jnp.concatenate(r_parts, axis=1) + vec_ref[1:2, :])) + k2)
    om = 1.0 - a * a
    b = (om * lax.rsqrt(jnp.maximum(om, SQRT_FLOOR))) * (ig * xc)

    groups = T // SUBLANES
    a = a.reshape(groups, SUBLANES, d)
    b = b.reshape(groups, SUBLANES, d)
    sub = lax.broadcasted_iota(jnp.int32, a.shape, 1)
    step = 1
    while step < SUBLANES:
        keep = sub >= step
        a_prev = jnp.where(keep, pltpu.roll(a, step, 1), 1.0)
        b_prev = jnp.where(keep, pltpu.roll(b, step, 1), 0.0)
        b = a * b_prev + b
        a = a * a_prev
        step *= 2
    carry = hc_ref[...]
    rows = []
    for g in range(groups):
        hg = b[g] + a[g] * carry
        carry = hg[SUBLANES - 1:SUBLANES, :]
        rows.append(hg)
    hc_ref[...] = carry
    y_ref[...] = (jnp.concatenate(rows, axis=0) * gate).astype(y_ref.dtype)


def _rglru_mixer(h, gain, w_in, conv_w, conv_b, w_a, b_a, w_x, b_x, lam, bsz, seq, tt=512):
    m, dm = h.shape
    d = w_in.shape[1] // 2
    nt = seq // tt
    row = lambda b, t: b * nt + t
    vecs = jnp.stack([conv_b, b_a, b_x, lam])
    return pl.pallas_call(
        _rglru_kernel,
        grid=(bsz, nt),
        in_specs=[pl.BlockSpec((tt, dm), lambda b, t: (row(b, t), 0)),
                  _const_spec((1, dm)),
                  _single_buffered(w_in.shape),
                  _const_spec(conv_w.shape),
                  _const_spec(vecs.shape),
                  _const_spec(w_a.shape),
                  _const_spec(w_x.shape)],
        out_specs=pl.BlockSpec((tt, d), lambda b, t: (row(b, t), 0)),
        out_shape=jax.ShapeDtypeStruct((m, d), MXU_DTYPE),
        scratch_shapes=[pltpu.VMEM((SUBLANES + tt, d), F32), pltpu.VMEM((1, d), F32)],
        compiler_params=_params("parallel", "arbitrary"),
        name="rglru_mixer",
    )(h, gain.reshape(1, dm), w_in.astype(MXU_DTYPE), conv_w, vecs, w_a.astype(MXU_DTYPE),
      w_x.astype(MXU_DTYPE))


def kernel(x, p, norm_gains, mlp_w_up, mlp_w_down, ple_w_up, ple_w_gate, a_w_in, a_ig_bias, a_fg_bias, a_head_gain, a_w_out, b_w_in, b_lower_bound, b_head_gain, b_w_out, c_w_in, c_ln_gain, c_ln_bias, c_spatial_w, c_spatial_b, c_w_out, d_w_in, d_conv_w, d_conv_b, d_w_a, d_b_a, d_w_x, d_b_x, d_lambda, d_w_out):
    bsz, seq, d = x.shape
    depth = norm_gains.shape[0]
    n_mixers = 4
    h = x.reshape(bsz * seq, d)
    for i in range(depth):
        kind, j = i % n_mixers, i // n_mixers
        g = norm_gains[i]
        if kind == 0:
            y = _mlstm_mixer(h, g[0], a_w_in[j], a_ig_bias[j], a_fg_bias[j], a_head_gain[j], bsz, seq)
            w_out = a_w_out[j]
        elif kind == 1:
            y = _hgrn2_mixer(h, g[0], b_w_in[j], b_lower_bound, b_head_gain[j], i, bsz, seq)
            w_out = b_w_out[j]
        elif kind == 2:
            y = _gmlp_mixer(h, g[0], c_w_in[j], c_ln_gain[j], c_ln_bias[j], c_spatial_w[j], c_spatial_b[j])
            w_out = c_w_out[j]
        else:
            y = _rglru_mixer(h, g[0], d_w_in[j], d_conv_w[j], d_conv_b[j], d_w_a[j], d_b_a[j], d_w_x[j],
                             d_b_x[j], d_lambda[j], bsz, seq)
            w_out = d_w_out[j]
        h = _post_mixer(h, y, p[i].reshape(bsz * seq, -1), g[1:5], w_out, mlp_w_up[i], mlp_w_down[i],
                        ple_w_gate[i], ple_w_up[i])
    return h.reshape(bsz, seq, d)
```

```python
import functools

import jax
import jax.numpy as jnp
from jax import lax
from jax.experimental import pallas as pl
from jax.experimental.pallas import tpu as pltpu

F32 = jnp.float32
MXU_DTYPE = jnp.bfloat16
NORM_EPS = 1e-6
RG_C = 8.0

LANES = 128
SUBLANES = 8
SQRT_FLOOR = 1e-30
LOG2_E = 1.4426950408889634
CHUNK = 128
VMEM_LIMIT = 63 * 1024 * 1024

A_HEADS, A_DQK, A_DV = 8, 64, 128
B_HEADS, B_DK = 8, 128
C_GROUPS = 8
D_BLOCKS, D_CONV = 4, 4


def _params(*sem):
    return pltpu.CompilerParams(dimension_semantics=sem, vmem_limit_bytes=VMEM_LIMIT)


def _mm(a, b):
    return jnp.dot(a.astype(MXU_DTYPE), b.astype(MXU_DTYPE), preferred_element_type=F32)


def _mm_nt(a, b):
    return lax.dot_general(a.astype(MXU_DTYPE), b.astype(MXU_DTYPE),
                           (((1,), (1,)), ((), ())), preferred_element_type=F32)


def _rms(x, gain):
    return x * lax.rsqrt(jnp.mean(x * x, axis=-1, keepdims=True) + NORM_EPS) * gain


def _sigmoid(x):
    return 0.5 * jnp.tanh(0.5 * x) + 0.5


def _log_sigmoid(x):
    return jnp.minimum(x, 0.0) - jnp.log(1.0 + jnp.exp(-jnp.abs(x)))


_GELU_K = 0.7978845608028654


def _gelu_tanh(x):
    hx = 0.5 * x
    return hx + hx * jnp.tanh(x * (_GELU_K + (_GELU_K * 0.044715) * (x * x)))


def _hgrn2_gates(fz, lbp, layer):
    e = jnp.exp(lbp - jnp.max(lbp, axis=0, keepdims=True))
    sm = e / jnp.sum(e, axis=0, keepdims=True)
    cs = sm[0:1, :]
    first = cs
    for r in range(1, layer + 1):
        cs = cs + sm[r:r + 1, :]
    lb = cs - first
    t = jnp.exp(-jnp.abs(fz))
    r = 1.0 / (1.0 + t)
    sig_abs, sig_nabs = r, t * r
    pos = fz >= 0.0
    sig = jnp.where(pos, sig_abs, sig_nabs)
    sig_neg = jnp.where(pos, sig_nabs, sig_abs)
    return jnp.log2(lb + (1.0 - lb) * sig), (1.0 - lb) * sig_neg


def _const_spec(shape):
    nd = len(shape)
    return pl.BlockSpec(shape, lambda *_: (0,) * nd)


def _single_buffered(shape):
    nd = len(shape)
    return pl.BlockSpec(shape, lambda *_: (0,) * nd, pipeline_mode=pl.Buffered(1))


def _post_mixer_kernel(h_ref, y_ref, p_ref, g_ref, wo_ref, wu_ref, wd_ref, wg_ref, wp_ref, o_ref,
                       *, ff_chunk, sub_rows):
    n_ff = wu_ref.shape[1] // ff_chunk
    n_sub = h_ref.shape[0] // sub_rows
    rows = [slice(r * sub_rows, (r + 1) * sub_rows) for r in range(n_sub)]

    def head(r):
        h = h_ref[rows[r], :] + _rms(jnp.dot(y_ref[rows[r], :], wo_ref[...], preferred_element_type=F32),
                                     g_ref[0:1, :])
        return h, _rms(h, g_ref[1:2, :]).astype(MXU_DTYPE), jnp.zeros(h.shape, F32)

    def ff(hn, acc, c):
        u = jnp.dot(hn, wu_ref[:, c * ff_chunk:(c + 1) * ff_chunk], preferred_element_type=F32)
        u = jnp.maximum(u, 0.0)
        u = (u * u).astype(MXU_DTYPE)
        return acc + jnp.dot(u, wd_ref[c * ff_chunk:(c + 1) * ff_chunk, :], preferred_element_type=F32)

    def tail(r, h, acc):
        h = h + _rms(acc, g_ref[2:3, :])
        gate = _sigmoid(jnp.dot(h.astype(MXU_DTYPE), wg_ref[...], preferred_element_type=F32))
        emb = jnp.dot(p_ref[rows[r], :].astype(MXU_DTYPE), wp_ref[...], preferred_element_type=F32)
        o_ref[rows[r], :] = h + _rms(gate * emb, g_ref[3:4, :])

    half = n_ff // 2
    prev = None
    for r in range(n_sub):
        h, hn, acc = head(r)
        if prev is not None:
            ph, phn, pacc = prev
            for c in range(half, n_ff):
                pacc = ff(phn, pacc, c)
        for c in range(half):
            acc = ff(hn, acc, c)
        if prev is not None:
            tail(r - 1, ph, pacc)
        prev = (h, hn, acc)
    ph, phn, pacc = prev
    for c in range(half, n_ff):
        pacc = ff(phn, pacc, c)
    tail(n_sub - 1, ph, pacc)


def _post_mixer(h, y, p, gains4, w_out, w_up, w_down, w_gate, w_ple, tm=1024, sub_rows=256, ff_chunk=1024):
    m, d = h.shape
    k = y.shape[1]
    dp = p.shape[1]
    weights = [w.astype(MXU_DTYPE) for w in (w_out, w_up, w_down, w_gate, w_ple)]
    return pl.pallas_call(
        functools.partial(_post_mixer_kernel, ff_chunk=ff_chunk, sub_rows=sub_rows),
        grid=(m // tm,),
        in_specs=[pl.BlockSpec((tm, d), lambda i: (i, 0)),
                  pl.BlockSpec((tm, k), lambda i: (i, 0)),
                  pl.BlockSpec((tm, dp), lambda i: (i, 0)),
                  _const_spec((4, d))] + [_single_buffered(w.shape) for w in weights],
        out_specs=pl.BlockSpec((tm, d), lambda i: (i, 0)),
        out_shape=jax.ShapeDtypeStruct((m, d), F32),
        compiler_params=_params("parallel"),
        name="post_mixer",
    )(h, y, p, gains4, *weights)


def _tril_mask(n):
    r = lax.broadcasted_iota(jnp.int32, (n, n), 0)
    c = lax.broadcasted_iota(jnp.int32, (n, n), 1)
    return r >= c


def _mlstm_gates(pre, m_ref):
    L = pre.shape[0]
    lane = lax.broadcasted_iota(jnp.int32, (L, LANES), 1)
    t_idx = lax.broadcasted_iota(jnp.int32, (L, LANES), 0)
    val = jnp.where(lane < A_HEADS, pre, jnp.where(lane < 2 * A_HEADS, _log_sigmoid(pre), 0.0))
    bc = jnp.dot(_tril_mask(L).astype(F32), val, preferred_element_type=F32,
                 precision=lax.Precision.HIGHEST)
    b = pltpu.roll(bc, LANES - A_HEADS, 1)
    a = val - b
    cm = a
    step = 1
    while step < L:
        cm = jnp.maximum(cm, jnp.where(t_idx >= step, pltpu.roll(cm, step, 0), -jnp.inf))
        step *= 2
    m_st = m_ref[...]
    mx = jnp.maximum(cm, m_st)
    mx_last = mx[L - 1:L, :]
    m_ref[...] = b[L - 1:L, :] + mx_last
    return dict(
        mx=mx,
        a_t=a.T,
        wk_t=jnp.exp(a - mx_last).T,
        decay=jnp.exp(m_st - mx_last),
        w_inter=jnp.exp(m_st - mx) * A_DQK ** -0.5,
        e_negm=jnp.exp(-(b + mx)))


def _mlstm_heads(gt, q, k, v, o_gate, hg_ref, c_ref, y_ref, rows):
    L = q.shape[0]
    causal = _tril_mask(L)
    scale = A_DQK ** -0.5
    k_t = k.T
    ones_col = (lax.broadcasted_iota(jnp.int32, (L, A_DV), 1) == 0).astype(MXU_DTYPE)
    heads = range(A_HEADS)
    qs = [q[:, hd * A_DQK:(hd + 1) * A_DQK].astype(MXU_DTYPE) for hd in heads]
    kts = [k_t[hd * A_DQK:(hd + 1) * A_DQK, :] for hd in heads]
    vs = [v[:, hd * A_DV:(hd + 1) * A_DV].astype(MXU_DTYPE) for hd in heads]
    cs = [c_ref[hd] for hd in heads]
    dmats = [jnp.exp(jnp.where(causal, gt["a_t"][hd:hd + 1, :] - gt["mx"][:, hd:hd + 1], -jnp.inf))
             for hd in heads]
    ss = [_mm(qs[hd], kts[hd]) * scale * dmats[hd] for hd in heads]
    inters = [_mm(qs[hd], cs[hd]) for hd in heads]
    for hd in heads:
        v_ext = jnp.concatenate([vs[hd], ones_col], axis=1)
        c_ref[hd] = (gt["decay"][:, hd:hd + 1] * cs[hd]
                     + _mm(kts[hd] * gt["wk_t"][hd:hd + 1, :], v_ext))
    wis = [gt["w_inter"][:, hd:hd + 1] for hd in heads]
    nums = [_mm(ss[hd], vs[hd]) + wis[hd] * inters[hd][:, :A_DV] for hd in heads]
    dens = [jnp.sum(ss[hd], axis=1, keepdims=True) + wis[hd] * inters[hd][:, A_DV:A_DV + 1] for hd in heads]
    outs = [nums[hd] * (1.0 / jnp.maximum(jnp.abs(dens[hd]), gt["e_negm"][:, hd:hd + 1])) for hd in heads]
    for hd in heads:
        sl = slice(hd * A_DV, (hd + 1) * A_DV)
        y_ref[rows, sl] = (o_gate[:, sl] * _rms(outs[hd], hg_ref[:, sl])).astype(y_ref.dtype)


def _mlstm_kernel(h_ref, g_ref, w_ref, bias_ref, hg_ref, y_ref, c_ref, m_ref):
    tm, d = y_ref.shape
    qk_w = A_HEADS * A_DQK

    @pl.when(pl.program_id(1) == 0)
    def _():
        c_ref[...] = jnp.zeros(c_ref.shape, F32)
        m_ref[...] = jnp.zeros(m_ref.shape, F32)

    hn = _rms(h_ref[...], g_ref[...]).astype(MXU_DTYPE)
    proj = lambda lo, hi: jnp.dot(hn, w_ref[:, lo:hi], preferred_element_type=F32)
    pre = proj(2 * qk_w + 2 * d, 2 * qk_w + 2 * d + LANES) + bias_ref[...]
    q = proj(0, qk_w)
    k = proj(qk_w, 2 * qk_w)
    chunks = [slice(c * CHUNK, (c + 1) * CHUNK) for c in range(tm // CHUNK)]
    gt = _mlstm_gates(pre[chunks[0], :], m_ref)
    v = proj(2 * qk_w, 2 * qk_w + d)
    o_gate = _sigmoid(proj(2 * qk_w + d, 2 * qk_w + 2 * d))
    for c, rows in enumerate(chunks):
        if c > 0:
            gt = _mlstm_gates(pre[rows, :], m_ref)
        _mlstm_heads(gt, q[rows, :], k[rows, :], v[rows, :], o_gate[rows, :], hg_ref, c_ref, y_ref, rows)


def _mlstm_mixer(h, gain, w_in, ig_bias, fg_bias, head_gain, bsz, seq, tm=2 * CHUNK):
    m, dm = h.shape
    d = A_HEADS * A_DV
    nt = seq // tm
    pad = LANES - 2 * A_HEADS
    w_in = jnp.pad(w_in, ((0, 0), (0, pad)))
    bias = jnp.pad(jnp.concatenate([ig_bias, fg_bias]), (0, pad)).reshape(1, LANES)
    return pl.pallas_call(
        _mlstm_kernel,
        grid=(bsz, nt),
        in_specs=[pl.BlockSpec((tm, dm), lambda b, t: (b * nt + t, 0)),
                  _const_spec((1, dm)),
                  _single_buffered(w_in.shape),
                  _const_spec((1, LANES)),
                  _const_spec((1, d))],
        out_specs=pl.BlockSpec((tm, d), lambda b, t: (b * nt + t, 0)),
        out_shape=jax.ShapeDtypeStruct((m, d), MXU_DTYPE),
        scratch_shapes=[pltpu.VMEM((A_HEADS, A_DQK, 2 * A_DV), F32),
                        pltpu.VMEM((1, LANES), F32)],
        compiler_params=_params("parallel", "arbitrary"),
        name="mlstm_mixer",
    )(h, gain.reshape(1, dm), w_in.astype(MXU_DTYPE), bias, head_gain.reshape(1, d))


def _block_ref_rows(g, w):
    L, D = g.shape
    if 2 * w >= SUBLANES:
        g3 = g.reshape(L // (2 * w), 2 * w, D)
        return jnp.broadcast_to(g3[:, w - 1:w, :], g3.shape).reshape(L, D)
    g3 = g.reshape(L // SUBLANES, SUBLANES, D)
    sub = lax.broadcasted_iota(jnp.int32, g3.shape, 1)
    out = None
    for start in range(0, SUBLANES, 2 * w):
        r = jnp.broadcast_to(g3[:, start + w - 1:start + w, :], g3.shape)
        out = r if out is None else jnp.where(sub >= start, r, out)
    return out.reshape(L, D)


def _split3(x):
    hi = x.astype(MXU_DTYPE)
    r1 = x - hi.astype(F32)
    mid = r1.astype(MXU_DTYPE)
    lo = (r1 - mid.astype(F32)).astype(MXU_DTYPE)
    return jnp.concatenate([hi, mid, lo], axis=1)


def _dot3(mat, x3, n):
    r = jnp.dot(mat, x3, preferred_element_type=F32)
    return r[:, 0:n] + r[:, n:2 * n] + r[:, 2 * n:3 * n]


def _hgrn2_attn(q, log2_f, k):
    L, D = q.shape
    dk = B_DK
    assert dk == L
    row = lax.broadcasted_iota(jnp.int32, (L, L), 0)
    col = lax.broadcasted_iota(jnp.int32, (L, L), 1)
    diag = row == col
    g = _dot3((row >= col).astype(MXU_DTYPE), _split3(log2_f), D)

    attn = []
    for hd in range(B_HEADS):
        sl = slice(hd * dk, (hd + 1) * dk)
        attn.append(jnp.where(diag, jnp.sum(q[:, sl] * k[:, sl], axis=1, keepdims=True), 0.0))
    w = L // 2
    while w >= 2:
        upper = (row % (2 * w)) >= w
        sign = jnp.where(upper, 1.0, -1.0)
        ref = _block_ref_rows(g, w)
        mask = ((row // (2 * w)) == (col // (2 * w))) & upper & ((col % (2 * w)) < w)
        for hd in range(B_HEADS):
            sl = slice(hd * dk, (hd + 1) * dk)
            x = jnp.exp2((g[:, sl] - ref[:, sl]) * sign)
            zh = (jnp.where(upper, q[:, sl], k[:, sl]) * x).astype(MXU_DTYPE)
            attn[hd] = jnp.where(mask, _mm_nt(zh, zh), attn[hd])
        w //= 2
    odd = (row % 2) == 1
    mask = ((row // 2) == (col // 2)) & odd & ((col % 2) == 0)
    for hd in range(B_HEADS):
        sl = slice(hd * dk, (hd + 1) * dk)
        zh = jnp.where(odd, q[:, sl] * jnp.exp2(log2_f[:, sl]), k[:, sl]).astype(MXU_DTYPE)
        attn[hd] = jnp.where(mask, _mm_nt(zh, zh), attn[hd])
    return attn, g


def _hgrn2_tail(attn, g, q, k, iv, silu_g, hg_ref, st_ref, y_ref, rows):
    L = q.shape[0]
    dk = B_DK
    g_last = g[L - 1:L, :]
    q_dec = (q * jnp.exp2(g)).astype(MXU_DTYPE)
    k_dec = (k * jnp.exp2(g_last - g)).astype(MXU_DTYPE)
    s_dec = jnp.exp2(g_last)
    heads = range(B_HEADS)
    sls = [slice(hd * dk, (hd + 1) * dk) for hd in heads]
    vs = [iv[:, sl] for sl in sls]
    sts = [st_ref[hd] for hd in heads]
    outs = [_mm(attn[hd], vs[hd]) + _mm_nt(q_dec[:, sls[hd]], sts[hd]) for hd in heads]
    for hd in heads:
        st_ref[hd] = s_dec[:, sls[hd]] * sts[hd] + _mm(vs[hd].T, k_dec[:, sls[hd]])
    for hd in heads:
        y = _rms(outs[hd], hg_ref[:, sls[hd]]) * silu_g[:, sls[hd]]
        y_ref[rows, sls[hd]] = y.astype(y_ref.dtype)


def _hgrn2_kernel(h_ref, g_ref, w_ref, lbp_ref, hg_ref, y_ref, st_ref, *, layer):
    tm, d = y_ref.shape

    @pl.when(pl.program_id(1) == 0)
    def _():
        st_ref[...] = jnp.zeros(st_ref.shape, F32)

    hn = _rms(h_ref[...], g_ref[...]).astype(MXU_DTYPE)
    proj = lambda j: jnp.dot(hn, w_ref[:, j * d:(j + 1) * d], preferred_element_type=F32)
    q = proj(0)
    log_f, k = _hgrn2_gates(proj(1), lbp_ref[...], layer)
    chunks = [slice(c * CHUNK, (c + 1) * CHUNK) for c in range(tm // CHUNK)]
    attn, g = _hgrn2_attn(q[chunks[0], :], log_f[chunks[0], :], k[chunks[0], :])
    iv = proj(2)
    gz = proj(3)
    silu_g = gz * _sigmoid(gz)
    for c, rows in enumerate(chunks):
        if c > 0:
            attn, g = _hgrn2_attn(q[rows, :], log_f[rows, :], k[rows, :])
        _hgrn2_tail(attn, g, q[rows, :], k[rows, :], iv[rows, :], silu_g[rows, :], hg_ref, st_ref, y_ref, rows)


def _hgrn2_mixer(h, gain, w_in, lower_bound, head_gain, layer, bsz, seq, tm=8 * CHUNK):
    m, dm = h.shape
    d = B_HEADS * B_DK
    nt = seq // tm
    return pl.pallas_call(
        functools.partial(_hgrn2_kernel, layer=layer),
        grid=(bsz, nt),
        in_specs=[pl.BlockSpec((tm, dm), lambda b, t: (b * nt + t, 0)),
                  _const_spec((1, dm)),
                  _single_buffered(w_in.shape),
                  _const_spec(lower_bound.shape),
                  _const_spec((1, d))],
        out_specs=pl.BlockSpec((tm, d), lambda b, t: (b * nt + t, 0)),
        out_shape=jax.ShapeDtypeStruct((m, d), MXU_DTYPE),
        scratch_shapes=[pltpu.VMEM((B_HEADS, B_DK, B_DK), F32)],
        compiler_params=_params("parallel", "arbitrary"),
        name="hgrn2_mixer",
    )(h, gain.reshape(1, dm), w_in.astype(MXU_DTYPE), lower_bound, head_gain.reshape(1, d))


def _gmlp_kernel(h_ref, g_ref, w_ref, lng_ref, lnb_ref, ws_ref, bs_ref, y_ref):
    tm = h_ref.shape[0]
    half = y_ref.shape[1]
    gw = half // C_GROUPS
    hn = _rms(h_ref[...], g_ref[...]).astype(MXU_DTYPE)
    v = _gelu_tanh(jnp.dot(hn, w_ref[:, half:], preferred_element_type=F32))
    u_group = lambda g: _gelu_tanh(jnp.dot(hn, w_ref[:, g * gw:(g + 1) * gw], preferred_element_type=F32))
    ahead = 2
    us = [u_group(g) for g in range(ahead)]
    mu = jnp.mean(v, axis=-1, keepdims=True)
    vc = v - mu
    var = jnp.mean(vc * vc, axis=-1, keepdims=True)
    vn = (vc * lax.rsqrt(var + NORM_EPS) * lng_ref[...] + lnb_ref[...]).astype(MXU_DTYPE)
    causal = _tril_mask(CHUNK)
    for g in range(C_GROUPS):
        sl = slice(g * gw, (g + 1) * gw)
        if g + ahead < C_GROUPS:
            us.append(u_group(g + ahead))
        u = us[g]
        w = jnp.where(causal, ws_ref[g], 0.0).astype(MXU_DTYPE)
        for c in range(tm // CHUNK):
            rows = slice(c * CHUNK, (c + 1) * CHUNK)
            vm = jnp.dot(w, vn[rows, sl], preferred_element_type=F32) + bs_ref[:, g:g + 1]
            y_ref[rows, sl] = (u[rows, :] * vm).astype(y_ref.dtype)


def _gmlp_mixer(h, gain, w_in, ln_gain, ln_bias, w_s, b_s, tm=1024):
    m, d = h.shape
    half = w_in.shape[1] // 2
    return pl.pallas_call(
        _gmlp_kernel,
        grid=(m // tm,),
        in_specs=[pl.BlockSpec((tm, d), lambda i: (i, 0)),
                  _const_spec((1, d)),
                  _single_buffered(w_in.shape),
                  _const_spec((1, half)),
                  _const_spec((1, half)),
                  _const_spec(w_s.shape),
                  _const_spec((CHUNK, C_GROUPS))],
        out_specs=pl.BlockSpec((tm, half), lambda i: (i, 0)),
        out_shape=jax.ShapeDtypeStruct((m, half), MXU_DTYPE),
        compiler_params=_params("parallel"),
        name="gmlp_mixer",
    )(h, gain.reshape(1, d), w_in.astype(MXU_DTYPE), ln_gain.reshape(1, half), ln_bias.reshape(1, half),
      w_s, b_s.T)


def _rglru_kernel(h_ref, g_ref, w_ref, cw_ref, vec_ref, wa_ref, wx_ref, y_ref, xpad_ref, hc_ref):
    T, d = y_ref.shape
    halo = xpad_ref.shape[0] - T

    @pl.when(pl.program_id(1) == 0)
    def _():
        xpad_ref[0:halo, :] = jnp.zeros((halo, d), F32)
        hc_ref[...] = jnp.zeros(hc_ref.shape, F32)

    hn = _rms(h_ref[...], g_ref[...]).astype(MXU_DTYPE)
    xb = jnp.dot(hn, w_ref[:, d:], preferred_element_type=F32)
    xpad_ref[halo:, :] = xb
    xc = xb * cw_ref[D_CONV - 1:D_CONV, :] + vec_ref[0:1, :]
    for j in range(1, D_CONV):
        shifted = xpad_ref[halo - j:halo - j + T, :]
        xc = xc + shifted * cw_ref[D_CONV - 1 - j:D_CONV - j, :]
    xpad_ref[0:halo, :] = xb[T - halo:T, :]

    bw = d // D_BLOCKS
    r_parts, i_parts = [], []
    for n in range(D_BLOCKS):
        xn = xc[:, n * bw:(n + 1) * bw]
        r_parts.append(_mm(xn, wa_ref[n]))
        i_parts.append(_mm(xn, wx_ref[n]))
    gate = _gelu_tanh(jnp.dot(hn, w_ref[:, :d], preferred_element_type=F32))
    ig = _sigmoid(jnp.concatenate(i_parts, axis=1) + vec_ref[2:3, :])
    nlam = -vec_ref[3:4, :]
    softplus = jnp.maximum(nlam, 0.0) + jnp.log1p(jnp.exp(-jnp.abs(nlam)))
    k2 = (-0.5 * RG_C * LOG2_E) * softplus
    a = jnp.exp2(k2 * jnp.tanh(0.5 * (jnp.concatenate(r_parts, axis=1) + vec_ref[1:2, :])) + k2)
    om = 1.0 - a * a
    b = (om * lax.rsqrt(jnp.maximum(om, SQRT_FLOOR))) * (ig * xc)

    groups = T // SUBLANES
    a = a.reshape(groups, SUBLANES, d)
    b = b.reshape(groups, SUBLANES, d)
    sub = lax.broadcasted_iota(jnp.int32, a.shape, 1)
    step = 1
    while step < SUBLANES:
        keep = sub >= step
        a_prev = jnp.where(keep, pltpu.roll(a, step, 1), 1.0)
        b_prev = jnp.where(keep, pltpu.roll(b, step, 1), 0.0)
        b = a * b_prev + b
        a = a * a_prev
        step *= 2
    carry = hc_ref[...]
    rows = []
    for g in range(groups):
        hg = b[g] + a[g] * carry
        carry = hg[SUBLANES - 1:SUBLANES, :]
        rows.append(hg)
    hc_ref[...] = carry
    y_ref[...] = (jnp.concatenate(rows, axis=0) * gate).astype(y_ref.dtype)


def _rglru_mixer(h, gain, w_in, conv_w, conv_b, w_a, b_a, w_x, b_x, lam, bsz, seq, tt=512):
    m, dm = h.shape
    d = w_in.shape[1] // 2
    nt = seq // tt
    row = lambda b, t: b * nt + t
    vecs = jnp.stack([conv_b, b_a, b_x, lam])
    return pl.pallas_call(
        _rglru_kernel,
        grid=(bsz, nt),
        in_specs=[pl.BlockSpec((tt, dm), lambda b, t: (row(b, t), 0)),
                  _const_spec((1, dm)),
                  _single_buffered(w_in.shape),
                  _const_spec(conv_w.shape),
                  _const_spec(vecs.shape),
                  _const_spec(w_a.shape),
                  _const_spec(w_x.shape)],
        out_specs=pl.BlockSpec((tt, d), lambda b, t: (row(b, t), 0)),
        out_shape=jax.ShapeDtypeStruct((m, d), MXU_DTYPE),
        scratch_shapes=[pltpu.VMEM((SUBLANES + tt, d), F32), pltpu.VMEM((1, d), F32)],
        compiler_params=_params("parallel", "arbitrary"),
        name="rglru_mixer",
    )(h, gain.reshape(1, dm), w_in.astype(MXU_DTYPE), conv_w, vecs, w_a.astype(MXU_DTYPE),
      w_x.astype(MXU_DTYPE))


def kernel(x, p, norm_gains, mlp_w_up, mlp_w_down, ple_w_up, ple_w_gate, a_w_in, a_ig_bias, a_fg_bias, a_head_gain, a_w_out, b_w_in, b_lower_bound, b_head_gain, b_w_out, c_w_in, c_ln_gain, c_ln_bias, c_spatial_w, c_spatial_b, c_w_out, d_w_in, d_conv_w, d_conv_b, d_w_a, d_b_a, d_w_x, d_b_x, d_lambda, d_w_out):
    bsz, seq, d = x.shape
    depth = norm_gains.shape[0]
    n_mixers = 4
    h = x.reshape(bsz * seq, d)
    for i in range(depth):
        kind, j = i % n_mixers, i // n_mixers
        g = norm_gains[i]
        if kind == 0:
            y = _mlstm_mixer(h, g[0], a_w_in[j], a_ig_bias[j], a_fg_bias[j], a_head_gain[j], bsz, seq)
            w_out = a_w_out[j]
        elif kind == 1:
            y = _hgrn2_mixer(h, g[0], b_w_in[j], b_lower_bound, b_head_gain[j], i, bsz, seq)
            w_out = b_w_out[j]
        elif kind == 2:
            y = _gmlp_mixer(h, g[0], c_w_in[j], c_ln_gain[j], c_ln_bias[j], c_spatial_w[j], c_spatial_b[j])
            w_out = c_w_out[j]
        else:
            y = _rglru_mixer(h, g[0], d_w_in[j], d_conv_w[j], d_conv_b[j], d_w_a[j], d_b_a[j], d_w_x[j],
                             d_b_x[j], d_lambda[j], bsz, seq)
            w_out = d_w_out[j]
        h = _post_mixer(h, y, p[i].reshape(bsz * seq, -1), g[1:5], w_out, mlp_w_up[i], mlp_w_down[i],
                        ple_w_gate[i], ple_w_up[i])
    return h.reshape(bsz, seq, d)
```

```python
import functools

import jax
import jax.numpy as jnp
from jax import lax
from jax.experimental import pallas as pl
from jax.experimental.pallas import tpu as pltpu

F32 = jnp.float32
MXU_DTYPE = jnp.bfloat16
NORM_EPS = 1e-6
RG_C = 8.0

LANES = 128
SUBLANES = 8
SQRT_FLOOR = 1e-30
LOG2_E = 1.4426950408889634
CHUNK = 128
VMEM_LIMIT = 63 * 1024 * 1024

A_HEADS, A_DQK, A_DV = 8, 64, 128
B_HEADS, B_DK = 8, 128
C_GROUPS = 8
D_BLOCKS, D_CONV = 4, 4


def _params(*sem):
    return pltpu.CompilerParams(dimension_semantics=sem, vmem_limit_bytes=VMEM_LIMIT)


def _mm(a, b):
    return jnp.dot(a.astype(MXU_DTYPE), b.astype(MXU_DTYPE), preferred_element_type=F32)


def _mm_nt(a, b):
    return lax.dot_general(a.astype(MXU_DTYPE), b.astype(MXU_DTYPE),
                           (((1,), (1,)), ((), ())), preferred_element_type=F32)


def _rms(x, gain):
    return x * lax.rsqrt(jnp.mean(x * x, axis=-1, keepdims=True) + NORM_EPS) * gain


def _sigmoid(x):
    return 0.5 * jnp.tanh(0.5 * x) + 0.5


def _log_sigmoid(x):
    return jnp.minimum(x, 0.0) - jnp.log(1.0 + jnp.exp(-jnp.abs(x)))


_GELU_K = 0.7978845608028654


def _gelu_tanh(x):
    hx = 0.5 * x
    return hx + hx * jnp.tanh(x * (_GELU_K + (_GELU_K * 0.044715) * (x * x)))


def _hgrn2_gates(fz, lbp, layer):
    e = jnp.exp(lbp - jnp.max(lbp, axis=0, keepdims=True))
    sm = e / jnp.sum(e, axis=0, keepdims=True)
    cs = sm[0:1, :]
    first = cs
    for r in range(1, layer + 1):
        cs = cs + sm[r:r + 1, :]
    lb = cs - first
    t = jnp.exp(-jnp.abs(fz))
    r = 1.0 / (1.0 + t)
    sig_abs, sig_nabs = r, t * r
    pos = fz >= 0.0
    sig = jnp.where(pos, sig_abs, sig_nabs)
    sig_neg = jnp.where(pos, sig_nabs, sig_abs)
    return jnp.log2(lb + (1.0 - lb) * sig), (1.0 - lb) * sig_neg


def _const_spec(shape):
    nd = len(shape)
    return pl.BlockSpec(shape, lambda *_: (0,) * nd)


def _single_buffered(shape):
    nd = len(shape)
    return pl.BlockSpec(shape, lambda *_: (0,) * nd, pipeline_mode=pl.Buffered(1))


def _post_mixer_kernel(h_ref, y_ref, p_ref, g_ref, wo_ref, wu_ref, wd_ref, wg_ref, wp_ref, o_ref,
                       *, ff_chunk, sub_rows):
    n_ff = wu_ref.shape[1] // ff_chunk
    n_sub = h_ref.shape[0] // sub_rows
    rows = [slice(r * sub_rows, (r + 1) * sub_rows) for r in range(n_sub)]

    def head(r):
        h = h_ref[rows[r], :] + _rms(jnp.dot(y_ref[rows[r], :], wo_ref[...], preferred_element_type=F32),
                                     g_ref[0:1, :])
        return h, _rms(h, g_ref[1:2, :]).astype(MXU_DTYPE), jnp.zeros(h.shape, F32)

    def ff(hn, acc, c):
        u = jnp.dot(hn, wu_ref[:, c * ff_chunk:(c + 1) * ff_chunk], preferred_element_type=F32)
        u = jnp.maximum(u, 0.0)
        u = (u * u).astype(MXU_DTYPE)
        return acc + jnp.dot(u, wd_ref[c * ff_chunk:(c + 1) * ff_chunk, :], preferred_element_type=F32)

    def tail(r, h, acc):
        h = h + _rms(acc, g_ref[2:3, :])
        gate = _sigmoid(jnp.dot(h.astype(MXU_DTYPE), wg_ref[...], preferred_element_type=F32))
        emb = jnp.dot(p_ref[rows[r], :].astype(MXU_DTYPE), wp_ref[...], preferred_element_type=F32)
        o_ref[rows[r], :] = h + _rms(gate * emb, g_ref[3:4, :])

    half = n_ff // 2
    prev = None
    for r in range(n_sub):
        h, hn, acc = head(r)
        if prev is not None:
            ph, phn, pacc = prev
            for c in range(half, n_ff):
                pacc = ff(phn, pacc, c)
        for c in range(half):
            acc = ff(hn, acc, c)
        if prev is not None:
            tail(r - 1, ph, pacc)
        prev = (h, hn, acc)
    ph, phn, pacc = prev
    for c in range(half, n_ff):
        pacc = ff(phn, pacc, c)
    tail(n_sub - 1, ph, pacc)


def _post_mixer(h, y, p, gains4, w_out, w_up, w_down, w_gate, w_ple, tm=1024, sub_rows=256, ff_chunk=1024):
    m, d = h.shape
    k = y.shape[1]
    dp = p.shape[1]
    weights = [w.astype(MXU_DTYPE) for w in (w_out, w_up, w_down, w_gate, w_ple)]
    return pl.pallas_call(
        functools.partial(_post_mixer_kernel, ff_chunk=ff_chunk, sub_rows=sub_rows),
        grid=(m // tm,),
        in_specs=[pl.BlockSpec((tm, d), lambda i: (i, 0)),
                  pl.BlockSpec((tm, k), lambda i: (i, 0)),
                  pl.BlockSpec((tm, dp), lambda i: (i, 0)),
                  _const_spec((4, d))] + [_single_buffered(w.shape) for w in weights],
        out_specs=pl.BlockSpec((tm, d), lambda i: (i, 0)),
        out_shape=jax.ShapeDtypeStruct((m, d), F32),
        compiler_params=_params("parallel"),
        name="post_mixer",
    )(h, y, p, gains4, *weights)


def _tril_mask(n):
    r = lax.broadcasted_iota(jnp.int32, (n, n), 0)
    c = lax.broadcasted_iota(jnp.int32, (n, n), 1)
    return r >= c


def _mlstm_gates(pre, m_ref):
    L = pre.shape[0]
    lane = lax.broadcasted_iota(jnp.int32, (L, LANES), 1)
    t_idx = lax.broadcasted_iota(jnp.int32, (L, LANES), 0)
    val = jnp.where(lane < A_HEADS, pre, jnp.where(lane < 2 * A_HEADS, _log_sigmoid(pre), 0.0))
    bc = jnp.dot(_tril_mask(L).astype(F32), val, preferred_element_type=F32,
                 precision=lax.Precision.HIGHEST)
    b = pltpu.roll(bc, LANES - A_HEADS, 1)
    a = val - b
    cm = a
    step = 1
    while step < L:
        cm = jnp.maximum(cm, jnp.where(t_idx >= step, pltpu.roll(cm, step, 0), -jnp.inf))
        step *= 2
    m_st = m_ref[...]
    mx = jnp.maximum(cm, m_st)
    mx_last = mx[L - 1:L, :]
    m_ref[...] = b[L - 1:L, :] + mx_last
    return dict(
        mx=mx,
        a_t=a.T,
        wk_t=jnp.exp(a - mx_last).T,
        decay=jnp.exp(m_st - mx_last),
        w_inter=jnp.exp(m_st - mx) * A_DQK ** -0.5,
        e_negm=jnp.exp(-(b + mx)))


def _mlstm_heads(gt, q, k, v, o_gate, hg_ref, c_ref, y_ref, rows):
    L = q.shape[0]
    causal = _tril_mask(L)
    scale = A_DQK ** -0.5
    k_t = k.T
    ones_col = (lax.broadcasted_iota(jnp.int32, (L, A_DV), 1) == 0).astype(MXU_DTYPE)
    heads = range(A_HEADS)
    qs = [q[:, hd * A_DQK:(hd + 1) * A_DQK].astype(MXU_DTYPE) for hd in heads]
    kts = [k_t[hd * A_DQK:(hd + 1) * A_DQK, :] for hd in heads]
    vs = [v[:, hd * A_DV:(hd + 1) * A_DV].astype(MXU_DTYPE) for hd in heads]
    cs = [c_ref[hd] for hd in heads]
    dmats = [jnp.exp(jnp.where(causal, gt["a_t"][hd:hd + 1, :] - gt["mx"][:, hd:hd + 1], -jnp.inf))
             for hd in heads]
    ss = [_mm(qs[hd], kts[hd]) * scale * dmats[hd] for hd in heads]
    inters = [_mm(qs[hd], cs[hd]) for hd in heads]
    for hd in heads:
        v_ext = jnp.concatenate([vs[hd], ones_col], axis=1)
        c_ref[hd] = (gt["decay"][:, hd:hd + 1] * cs[hd]
                     + _mm(kts[hd] * gt["wk_t"][hd:hd + 1, :], v_ext))
    wis = [gt["w_inter"][:, hd:hd + 1] for hd in heads]
    nums = [_mm(ss[hd], vs[hd]) + wis[hd] * inters[hd][:, :A_DV] for hd in heads]
    dens = [jnp.sum(ss[hd], axis=1, keepdims=True) + wis[hd] * inters[hd][:, A_DV:A_DV + 1] for hd in heads]
    outs = [nums[hd] * (1.0 / jnp.maximum(jnp.abs(dens[hd]), gt["e_negm"][:, hd:hd + 1])) for hd in heads]
    for hd in heads:
        sl = slice(hd * A_DV, (hd + 1) * A_DV)
        y_ref[rows, sl] = (o_gate[:, sl] * _rms(outs[hd], hg_ref[:, sl])).astype(y_ref.dtype)


def _mlstm_kernel(h_ref, g_ref, w_ref, bias_ref, hg_ref, y_ref, c_ref, m_ref):
    tm, d = y_ref.shape
    qk_w = A_HEADS * A_DQK

    @pl.when(pl.program_id(1) == 0)
    def _():
        c_ref[...] = jnp.zeros(c_ref.shape, F32)
        m_ref[...] = jnp.zeros(m_ref.shape, F32)

    hn = _rms(h_ref[...], g_ref[...]).astype(MXU_DTYPE)
    proj = lambda lo, hi: jnp.dot(hn, w_ref[:, lo:hi], preferred_element_type=F32)
    pre = proj(2 * qk_w + 2 * d, 2 * qk_w + 2 * d + LANES) + bias_ref[...]
    q = proj(0, qk_w)
    k = proj(qk_w, 2 * qk_w)
    chunks = [slice(c * CHUNK, (c + 1) * CHUNK) for c in range(tm // CHUNK)]
    gt = _mlstm_gates(pre[chunks[0], :], m_ref)
    v = proj(2 * qk_w, 2 * qk_w + d)
    o_gate = _sigmoid(proj(2 * qk_w + d, 2 * qk_w + 2 * d))
    for c, rows in enumerate(chunks):
        if c > 0:
            gt = _mlstm_gates(pre[rows, :], m_ref)
        _mlstm_heads(gt, q[rows, :], k[rows, :], v[rows, :], o_gate[rows, :], hg_ref, c_ref, y_ref, rows)


def _mlstm_mixer(h, gain, w_in, ig_bias, fg_bias, head_gain, bsz, seq, tm=4 * CHUNK):
    m, dm = h.shape
    d = A_HEADS * A_DV
    nt = seq // tm
    pad = LANES - 2 * A_HEADS
    w_in = jnp.pad(w_in, ((0, 0), (0, pad)))
    bias = jnp.pad(jnp.concatenate([ig_bias, fg_bias]), (0, pad)).reshape(1, LANES)
    return pl.pallas_call(
        _mlstm_kernel,
        grid=(bsz, nt),
        in_specs=[pl.BlockSpec((tm, dm), lambda b, t: (b * nt + t, 0)),
                  _const_spec((1, dm)),
                  _single_buffered(w_in.shape),
                  _const_spec((1, LANES)),
                  _const_spec((1, d))],
        out_specs=pl.BlockSpec((tm, d), lambda b, t: (b * nt + t, 0)),
        out_shape=jax.ShapeDtypeStruct((m, d), MXU_DTYPE),
        scratch_shapes=[pltpu.VMEM((A_HEADS, A_DQK, 2 * A_DV), F32),
                        pltpu.VMEM((1, LANES), F32)],
        compiler_params=_params("parallel", "arbitrary"),
        name="mlstm_mixer",
    )(h, gain.reshape(1, dm), w_in.astype(MXU_DTYPE), bias, head_gain.reshape(1, d))


def _block_ref_rows(g, w):
    L, D = g.shape
    if 2 * w >= SUBLANES:
        g3 = g.reshape(L // (2 * w), 2 * w, D)
        return jnp.broadcast_to(g3[:, w - 1:w, :], g3.shape).reshape(L, D)
    g3 = g.reshape(L // SUBLANES, SUBLANES, D)
    sub = lax.broadcasted_iota(jnp.int32, g3.shape, 1)
    out = None
    for start in range(0, SUBLANES, 2 * w):
        r = jnp.broadcast_to(g3[:, start + w - 1:start + w, :], g3.shape)
        out = r if out is None else jnp.where(sub >= start, r, out)
    return out.reshape(L, D)


def _split3(x):
    hi = x.astype(MXU_DTYPE)
    r1 = x - hi.astype(F32)
    mid = r1.astype(MXU_DTYPE)
    lo = (r1 - mid.astype(F32)).astype(MXU_DTYPE)
    return jnp.concatenate([hi, mid, lo], axis=1)


def _dot3(mat, x3, n):
    r = jnp.dot(mat, x3, preferred_element_type=F32)
    return r[:, 0:n] + r[:, n:2 * n] + r[:, 2 * n:3 * n]


def _hgrn2_attn(q, log2_f, k):
    L, D = q.shape
    dk = B_DK
    assert dk == L
    row = lax.broadcasted_iota(jnp.int32, (L, L), 0)
    col = lax.broadcasted_iota(jnp.int32, (L, L), 1)
    diag = row == col
    g = _dot3((row >= col).astype(MXU_DTYPE), _split3(log2_f), D)

    attn = []
    for hd in range(B_HEADS):
        sl = slice(hd * dk, (hd + 1) * dk)
        attn.append(jnp.where(diag, jnp.sum(q[:, sl] * k[:, sl], axis=1, keepdims=True), 0.0))
    w = L // 2
    while w >= 2:
        upper = (row % (2 * w)) >= w
        sign = jnp.where(upper, 1.0, -1.0)
        ref = _block_ref_rows(g, w)
        mask = ((row // (2 * w)) == (col // (2 * w))) & upper & ((col % (2 * w)) < w)
        for hd in range(B_HEADS):
            sl = slice(hd * dk, (hd + 1) * dk)
            x = jnp.exp2((g[:, sl] - ref[:, sl]) * sign)
            zh = (jnp.where(upper, q[:, sl], k[:, sl]) * x).astype(MXU_DTYPE)
            attn[hd] = jnp.where(mask, _mm_nt(zh, zh), attn[hd])
        w //= 2
    odd = (row % 2) == 1
    mask = ((row // 2) == (col // 2)) & odd & ((col % 2) == 0)
    for hd in range(B_HEADS):
        sl = slice(hd * dk, (hd + 1) * dk)
        zh = jnp.where(odd, q[:, sl] * jnp.exp2(log2_f[:, sl]), k[:, sl]).astype(MXU_DTYPE)
        attn[hd] = jnp.where(mask, _mm_nt(zh, zh), attn[hd])
    return attn, g


def _hgrn2_tail(attn, g, q, k, iv, silu_g, hg_ref, st_ref, y_ref, rows):
    L = q.shape[0]
    dk = B_DK
    g_last = g[L - 1:L, :]
    q_dec = (q * jnp.exp2(g)).astype(MXU_DTYPE)
    k_dec = (k * jnp.exp2(g_last - g)).astype(MXU_DTYPE)
    s_dec = jnp.exp2(g_last)
    heads = range(B_HEADS)
    sls = [slice(hd * dk, (hd + 1) * dk) for hd in heads]
    vs = [iv[:, sl] for sl in sls]
    sts = [st_ref[hd] for hd in heads]
    outs = [_mm(attn[hd], vs[hd]) + _mm_nt(q_dec[:, sls[hd]], sts[hd]) for hd in heads]
    for hd in heads:
        st_ref[hd] = s_dec[:, sls[hd]] * sts[hd] + _mm(vs[hd].T, k_dec[:, sls[hd]])
    for hd in heads:
        y = _rms(outs[hd], hg_ref[:, sls[hd]]) * silu_g[:, sls[hd]]
        y_ref[rows, sls[hd]] = y.astype(y_ref.dtype)


def _hgrn2_kernel(h_ref, g_ref, w_ref, lbp_ref, hg_ref, y_ref, st_ref, *, layer):
    tm, d = y_ref.shape

    @pl.when(pl.program_id(1) == 0)
    def _():
        st_ref[...] = jnp.zeros(st_ref.shape, F32)

    hn = _rms(h_ref[...], g_ref[...]).astype(MXU_DTYPE)
    proj = lambda j: jnp.dot(hn, w_ref[:, j * d:(j + 1) * d], preferred_element_type=F32)
    q = proj(0)
    log_f, k = _hgrn2_gates(proj(1), lbp_ref[...], layer)
    chunks = [slice(c * CHUNK, (c + 1) * CHUNK) for c in range(tm // CHUNK)]
    attn, g = _hgrn2_attn(q[chunks[0], :], log_f[chunks[0], :], k[chunks[0], :])
    iv = proj(2)
    gz = proj(3)
    silu_g = gz * _sigmoid(gz)
    for c, rows in enumerate(chunks):
        if c > 0:
            attn, g = _hgrn2_attn(q[rows, :], log_f[rows, :], k[rows, :])
        _hgrn2_tail(attn, g, q[rows, :], k[rows, :], iv[rows, :], silu_g[rows, :], hg_ref, st_ref, y_ref, rows)


def _hgrn2_mixer(h, gain, w_in, lower_bound, head_gain, layer, bsz, seq, tm=8 * CHUNK):
    m, dm = h.shape
    d = B_HEADS * B_DK
    nt = seq // tm
    return pl.pallas_call(
        functools.partial(_hgrn2_kernel, layer=layer),
        grid=(bsz, nt),
        in_specs=[pl.BlockSpec((tm, dm), lambda b, t: (b * nt + t, 0)),
                  _const_spec((1, dm)),
                  _single_buffered(w_in.shape),
                  _const_spec(lower_bound.shape),
                  _const_spec((1, d))],
        out_specs=pl.BlockSpec((tm, d), lambda b, t: (b * nt + t, 0)),
        out_shape=jax.ShapeDtypeStruct((m, d), MXU_DTYPE),
        scratch_shapes=[pltpu.VMEM((B_HEADS, B_DK, B_DK), F32)],
        compiler_params=_params("parallel", "arbitrary"),
        name="hgrn2_mixer",
    )(h, gain.reshape(1, dm), w_in.astype(MXU_DTYPE), lower_bound, head_gain.reshape(1, d))


def _gmlp_kernel(h_ref, g_ref, w_ref, lng_ref, lnb_ref, ws_ref, bs_ref, y_ref):
    tm = h_ref.shape[0]
    half = y_ref.shape[1]
    gw = half // C_GROUPS
    hn = _rms(h_ref[...], g_ref[...]).astype(MXU_DTYPE)
    v = _gelu_tanh(jnp.dot(hn, w_ref[:, half:], preferred_element_type=F32))
    u_group = lambda g: _gelu_tanh(jnp.dot(hn, w_ref[:, g * gw:(g + 1) * gw], preferred_element_type=F32))
    ahead = 2
    us = [u_group(g) for g in range(ahead)]
    mu = jnp.mean(v, axis=-1, keepdims=True)
    vc = v - mu
    var = jnp.mean(vc * vc, axis=-1, keepdims=True)
    vn = (vc * lax.rsqrt(var + NORM_EPS) * lng_ref[...] + lnb_ref[...]).astype(MXU_DTYPE)
    causal = _tril_mask(CHUNK)
    for g in range(C_GROUPS):
        sl = slice(g * gw, (g + 1) * gw)
        if g + ahead < C_GROUPS:
            us.append(u_group(g + ahead))
        u = us[g]
        w = jnp.where(causal, ws_ref[g], 0.0).astype(MXU_DTYPE)
        for c in range(tm // CHUNK):
            rows = slice(c * CHUNK, (c + 1) * CHUNK)
            vm = jnp.dot(w, vn[rows, sl], preferred_element_type=F32) + bs_ref[:, g:g + 1]
            y_ref[rows, sl] = (u[rows, :] * vm).astype(y_ref.dtype)


def _gmlp_mixer(h, gain, w_in, ln_gain, ln_bias, w_s, b_s, tm=1024):
    m, d = h.shape
    half = w_in.shape[1] // 2
    return pl.pallas_call(
        _gmlp_kernel,
        grid=(m // tm,),
        in_specs=[pl.BlockSpec((tm, d), lambda i: (i, 0)),
                  _const_spec((1, d)),
                  _single_buffered(w_in.shape),
                  _const_spec((1, half)),
                  _const_spec((1, half)),
                  _const_spec(w_s.shape),
                  _const_spec((CHUNK, C_GROUPS))],
        out_specs=pl.BlockSpec((tm, half), lambda i: (i, 0)),
        out_shape=jax.ShapeDtypeStruct((m, half), MXU_DTYPE),
        compiler_params=_params("parallel"),
        name="gmlp_mixer",
    )(h, gain.reshape(1, d), w_in.astype(MXU_DTYPE), ln_gain.reshape(1, half), ln_bias.reshape(1, half),
      w_s, b_s.T)


def _rglru_kernel(h_ref, g_ref, w_ref, cw_ref, vec_ref, wa_ref, wx_ref, y_ref, xpad_ref, hc_ref):
    T, d = y_ref.shape
    halo = xpad_ref.shape[0] - T

    @pl.when(pl.program_id(1) == 0)
    def _():
        xpad_ref[0:halo, :] = jnp.zeros((halo, d), F32)
        hc_ref[...] = jnp.zeros(hc_ref.shape, F32)

    hn = _rms(h_ref[...], g_ref[...]).astype(MXU_DTYPE)
    xb = jnp.dot(hn, w_ref[:, d:], preferred_element_type=F32)
    xpad_ref[halo:, :] = xb
    xc = xb * cw_ref[D_CONV - 1:D_CONV, :] + vec_ref[0:1, :]
    for j in range(1, D_CONV):
        shifted = xpad_ref[halo - j:halo - j + T, :]
        xc = xc + shifted * cw_ref[D_CONV - 1 - j:D_CONV - j, :]
    xpad_ref[0:halo, :] = xb[T - halo:T, :]

    bw = d // D_BLOCKS
    r_parts, i_parts = [], []
    for n in range(D_BLOCKS):
        xn = xc[:, n * bw:(n + 1) * bw]
        r_parts.append(_mm(xn, wa_ref[n]))
        i_parts.append(_mm(xn, wx_ref[n]))
    gate = _gelu_tanh(jnp.dot(hn, w_ref[:, :d], preferred_element_type=F32))
    ig = _sigmoid(jnp.concatenate(i_parts, axis=1) + vec_ref[2:3, :])
    nlam = -vec_ref[3:4, :]
    softplus = jnp.maximum(nlam, 0.0) + jnp.log1p(jnp.exp(-jnp.abs(nlam)))
    k2 = (-0.5 * RG_C * LOG2_E) * softplus
    a = jnp.exp2(k2 * jnp.tanh(0.5 * (jnp.concatenate(r_parts, axis=1) + vec_ref[1:2, :])) + k2)
    om = 1.0 - a * a
    b = (om * lax.rsqrt(jnp.maximum(om, SQRT_FLOOR))) * (ig * xc)

    groups = T // SUBLANES
    a = a.reshape(groups, SUBLANES, d)
    b = b.reshape(groups, SUBLANES, d)
    sub = lax.broadcasted_iota(jnp.int32, a.shape, 1)
    step = 1
    while step < SUBLANES:
        keep = sub >= step
        a_prev = jnp.where(keep, pltpu.roll(a, step, 1), 1.0)
        b_prev = jnp.where(keep, pltpu.roll(b, step, 1), 0.0)
        b = a * b_prev + b
        a = a * a_prev
        step *= 2
    carry = hc_ref[...]
    rows = []
    for g in range(groups):
        hg = b[g] + a[g] * carry
        carry = hg[SUBLANES - 1:SUBLANES, :]
        rows.append(hg)
    hc_ref[...] = carry
    y_ref[...] = (jnp.concatenate(rows, axis=0) * gate).astype(y_ref.dtype)


def _rglru_mixer(h, gain, w_in, conv_w, conv_b, w_a, b_a, w_x, b_x, lam, bsz, seq, tt=1024):
    m, dm = h.shape
    d = w_in.shape[1] // 2
    nt = seq // tt
    row = lambda b, t: b * nt + t
    vecs = jnp.stack([conv_b, b_a, b_x, lam])
    return pl.pallas_call(
        _rglru_kernel,
        grid=(bsz, nt),
        in_specs=[pl.BlockSpec((tt, dm), lambda b, t: (row(b, t), 0)),
                  _const_spec((1, dm)),
                  _single_buffered(w_in.shape),
                  _const_spec(conv_w.shape),
                  _const_spec(vecs.shape),
                  _const_spec(w_a.shape),
                  _const_spec(w_x.shape)],
        out_specs=pl.BlockSpec((tt, d), lambda b, t: (row(b, t), 0)),
        out_shape=jax.ShapeDtypeStruct((m, d), MXU_DTYPE),
        scratch_shapes=[pltpu.VMEM((SUBLANES + tt, d), F32), pltpu.VMEM((1, d), F32)],
        compiler_params=_params("parallel", "arbitrary"),
        name="rglru_mixer",
    )(h, gain.reshape(1, dm), w_in.astype(MXU_DTYPE), conv_w, vecs, w_a.astype(MXU_DTYPE),
      w_x.astype(MXU_DTYPE))


def kernel(x, p, norm_gains, mlp_w_up, mlp_w_down, ple_w_up, ple_w_gate, a_w_in, a_ig_bias, a_fg_bias, a_head_gain, a_w_out, b_w_in, b_lower_bound, b_head_gain, b_w_out, c_w_in, c_ln_gain, c_ln_bias, c_spatial_w, c_spatial_b, c_w_out, d_w_in, d_conv_w, d_conv_b, d_w_a, d_b_a, d_w_x, d_b_x, d_lambda, d_w_out):
    bsz, seq, d = x.shape
    depth = norm_gains.shape[0]
    n_mixers = 4
    h = x.reshape(bsz * seq, d)
    for i in range(depth):
        kind, j = i % n_mixers, i // n_mixers
        g = norm_gains[i]
        if kind == 0:
            y = _mlstm_mixer(h, g[0], a_w_in[j], a_ig_bias[j], a_fg_bias[j], a_head_gain[j], bsz, seq)
            w_out = a_w_out[j]
        elif kind == 1:
            y = _hgrn2_mixer(h, g[0], b_w_in[j], b_lower_bound, b_head_gain[j], i, bsz, seq)
            w_out = b_w_out[j]
        elif kind == 2:
            y = _gmlp_mixer(h, g[0], c_w_in[j], c_ln_gain[j], c_ln_bias[j], c_spatial_w[j], c_spatial_b[j])
            w_out = c_w_out[j]
        else:
            y = _rglru_mixer(h, g[0], d_w_in[j], d_conv_w[j], d_conv_b[j], d_w_a[j], d_b_a[j], d_w_x[j],
                             d_b_x[j], d_lambda[j], bsz, seq)
            w_out = d_w_out[j]
        h = _post_mixer(h, y, p[i].reshape(bsz * seq, -1), g[1:5], w_out, mlp_w_up[i], mlp_w_down[i],
                        ple_w_gate[i], ple_w_up[i])
    return h.reshape(bsz, seq, d)
```
